```python
import jax, jax.numpy as jnp
from jax import lax
import numpy as np

D_MODEL = 1024
BATCH = 32
SEQ = 256
DEPTH = 4
DEC_BATCH = 8
DEC_SEQ = 4096
PAST_LEN = 512

GRID_W = 64
N_MIXERS = 3
N_HEADS = 16
N_KV_HEADS = 4
HEAD_DIM = D_MODEL // N_HEADS
Q_PER_KV = N_HEADS // N_KV_HEADS
QKV_DIM = (N_HEADS + 2 * N_KV_HEADS) * HEAD_DIM
WINDOW = 128
BLOCK = 128
ROPE_THETA = 10000.0
POOL_WINDOWS = (2, 4, 8, 16)
N_POOL_GROUPS = 4
POOL_GROUP_DIM = D_MODEL // N_POOL_GROUPS
N_FFT_GROUPS = 4
FFT_GROUP_DIM = D_MODEL // N_FFT_GROUPS
D_FF = ((8 * D_MODEL // 3 + 127) // 128) * 128
CONV_W = 3
EPS = 1e-6
NEG_INF = -1e30
N_ATTN_LAYERS = (DEPTH + 2) // 3
N_POOL_LAYERS = (DEPTH + 1) // 3
N_FFT_LAYERS = DEPTH // 3

kernel_name = 'hybrid_diffusion_prefix_trunk_step'


def rmsnorm(x, g):
    xf = x.astype(jnp.float32)
    y = xf * lax.rsqrt(jnp.mean(xf * xf, axis=-1, keepdims=True) + EPS)
    return (y * g.astype(jnp.float32)).astype(x.dtype)


def ada_mods(cond, w, b):
    cond = cond.reshape(-1, D_MODEL)
    mod = (jax.nn.silu(cond) @ w + b)[:, None, :]
    return jnp.split(mod, 6, axis=-1)


def modulate(h, shift, scale):
    return h * (1 + scale) + shift


def axial_rope(x):
    L = x.shape[1]
    rows_n = L // GRID_W
    rows = jnp.repeat(jnp.arange(rows_n), GRID_W)
    cols = jnp.tile(jnp.arange(GRID_W), rows_n)
    half = HEAD_DIM // 2
    inv_freq = 1.0 / (ROPE_THETA ** (jnp.arange(0, half, 2, dtype=jnp.float32) / half))

    def rot(xp, pos):
        ang = pos.astype(jnp.float32)[:, None] * inv_freq
        cos = jnp.cos(ang)[None, :, None, :]
        sin = jnp.sin(ang)[None, :, None, :]
        x1, x2 = jnp.split(xp.astype(jnp.float32), 2, axis=-1)
        return jnp.concatenate([x1 * cos - x2 * sin, x1 * sin + x2 * cos], axis=-1)

    return jnp.concatenate([rot(x[..., :half], rows), rot(x[..., half:], cols)], axis=-1).astype(x.dtype)


def qkv_proj(h, w_qkv, q_norm, k_norm):
    B, L, _ = h.shape
    q, k, v = jnp.split(h @ w_qkv, [N_HEADS * HEAD_DIM, (N_HEADS + N_KV_HEADS) * HEAD_DIM], axis=-1)
    q = rmsnorm(q.reshape(B, L, N_HEADS, HEAD_DIM), q_norm)
    k = rmsnorm(k.reshape(B, L, N_KV_HEADS, HEAD_DIM), k_norm)
    v = v.reshape(B, L, N_KV_HEADS, HEAD_DIM)
    return q, k, v


def attend_block(qb, keys, values, masks, sink):
    scale = HEAD_DIM ** -0.5
    scores = []
    for kp, mk in zip(keys, masks):
        s = jnp.einsum('bqkgd,bskd->bkgqs', qb, kp).astype(jnp.float32) * scale
        if mk is not None:
            s = jnp.where(mk, s, NEG_INF)
        scores.append(s)
    sink_f = sink.astype(jnp.float32).reshape(1, N_KV_HEADS, Q_PER_KV, 1)
    m = sink_f
    for s in scores:
        m = jnp.maximum(m, jnp.max(s, axis=-1))
    denom = jnp.exp(sink_f - m)
    out = jnp.zeros(qb.shape, jnp.float32)
    for s, vp in zip(scores, values):
        p = jnp.exp(s - m[..., None])
        denom = denom + jnp.sum(p, axis=-1)
        out = out + jnp.einsum('bkgqs,bskd->bqkgd', p.astype(vp.dtype), vp).astype(jnp.float32)
    out = out / jnp.moveaxis(denom, -1, 1)[..., None]
    return out.astype(qb.dtype)


def context_attention(q, k, v, sink):
    B, S = q.shape[:2]
    nqb = S // BLOCK
    qg = jnp.moveaxis(q.reshape(B, nqb, BLOCK, N_KV_HEADS, Q_PER_KV, HEAD_DIM), 1, 0)
    out = lax.map(lambda qb: attend_block(qb, [k], [v], [None], sink), qg)
    return jnp.moveaxis(out, 0, 1).reshape(B, S, N_HEADS * HEAD_DIM)


def latent_attention(q, k, v, ctx_k, ctx_v, sink):
    B, L = q.shape[:2]
    nqb = L // BLOCK
    qg = jnp.moveaxis(q.reshape(B, nqb, BLOCK, N_KV_HEADS, Q_PER_KV, HEAD_DIM), 1, 0)
    pad = ((0, 0), (BLOCK, BLOCK), (0, 0), (0, 0))
    k_pad = jnp.pad(k, pad)
    v_pad = jnp.pad(v, pad)
    qi = jnp.arange(BLOCK)
    sj = jnp.arange(3 * BLOCK)
    band = jnp.abs((sj[None, :] - BLOCK) - qi[:, None]) <= WINDOW

    def one_block(args):
        n, qb = args
        kw = lax.dynamic_slice_in_dim(k_pad, n * BLOCK, 3 * BLOCK, axis=1)
        vw = lax.dynamic_slice_in_dim(v_pad, n * BLOCK, 3 * BLOCK, axis=1)
        kpos = n * BLOCK - BLOCK + sj
        mask = band & ((kpos >= 0) & (kpos < L))[None, :]
        return attend_block(qb, [kw, ctx_k], [vw, ctx_v], [mask, None], sink)

    out = lax.map(one_block, (jnp.arange(nqb), qg))
    return jnp.moveaxis(out, 0, 1).reshape(B, L, N_HEADS * HEAD_DIM)


def pool_mix(h, w, scale):
    B, L, D = h.shape
    hf = h.astype(jnp.float32)
    cs = jnp.concatenate([jnp.zeros((B, 1, D), jnp.float32), jnp.cumsum(hf, axis=1)], axis=1)
    t = jnp.arange(L)
    outs = []
    for g, win in enumerate(POOL_WINDOWS):
        lo = win // 2
        hi = win - lo - 1
        start = jnp.clip(t - lo, 0, L)
        end = jnp.clip(t + hi + 1, 0, L)
        csg = cs[..., g * POOL_GROUP_DIM:(g + 1) * POOL_GROUP_DIM]
        mean = (csg[:, end] - csg[:, start]) / (end - start).astype(jnp.float32)[None, :, None]
        outs.append(mean - hf[..., g * POOL_GROUP_DIM:(g + 1) * POOL_GROUP_DIM])
    pooled = jnp.stack(outs, axis=2).astype(h.dtype)
    mixed = jnp.einsum('blgc,gcd->blgd', pooled, w).reshape(B, L, D)
    return mixed * scale


def fourier_mix(h, w):
    B, L, D = h.shape
    hg = h.astype(jnp.float32).reshape(B, L, N_FFT_GROUPS, FFT_GROUP_DIM)
    f = jnp.fft.fft2(hg, axes=(1, 3), norm='ortho').real
    return f.reshape(B, L, D).astype(h.dtype) @ w


def conv_ffn(h, w_in, conv_w, conv_b, w_out):
    u = h @ w_in
    up = jnp.pad(u, ((0, 0), (1, 1), (0, 0)))
    u = up[:, :-2] * conv_w[0] + up[:, 1:-1] * conv_w[1] + up[:, 2:] * conv_w[2] + conv_b
    gate, val = jnp.split(u, 2, axis=-1)
    return (jax.nn.silu(gate) * val) @ w_out


def trunk(x, cond, ctx_k_cache, ctx_v_cache, weights):
    (norm_mix, norm_ffn, ada_w, ada_b, attn_w_qkv, attn_q_norm, attn_k_norm, attn_sink, attn_w_o,
     pool_w, pool_scale, fnet_w, ffn_w_in, ffn_conv_w, ffn_conv_b, ffn_w_out) = weights
    is_latent = ctx_k_cache is not None
    ks, vs = [], []
    for i in range(DEPTH):
        kind, j = i % N_MIXERS, i // N_MIXERS
        sh1, sc1, g1, sh2, sc2, g2 = ada_mods(cond, ada_w[i], ada_b[i])
        h = modulate(rmsnorm(x, norm_mix[i]), sh1, sc1)
        if kind == 0:
            q, k, v = qkv_proj(h, attn_w_qkv[j], attn_q_norm[j], attn_k_norm[j])
            if is_latent:
                o = latent_attention(axial_rope(q), axial_rope(k), v,
                                     ctx_k_cache[:, j], ctx_v_cache[:, j], attn_sink[j])
            else:
                o = context_attention(q, k, v, attn_sink[j])
                ks.append(k)
                vs.append(v)
            mix = o @ attn_w_o[j]
        elif kind == 1:
            mix = pool_mix(h, pool_w[j], pool_scale[j])
        else:
            mix = fourier_mix(h, fnet_w[j])
        x = x + g1 * mix
        h = modulate(rmsnorm(x, norm_ffn[i]), sh2, sc2)
        x = x + g2 * conv_ffn(h, ffn_w_in[i], ffn_conv_w[i], ffn_conv_b[i], ffn_w_out[i])
    return x, ks, vs


def setup_inputs(seed: int = 0) -> dict:
    key = jax.random.key(seed)
    ks = jax.random.split(key, 24)

    def nrm(k, shape, scale=1.0):
        return jax.random.normal(k, shape, jnp.float32) * scale

    cache_shape = (DEC_BATCH, N_ATTN_LAYERS, PAST_LEN, N_KV_HEADS, HEAD_DIM)
    return {
        'x_prompt': nrm(ks[0], (BATCH, SEQ, D_MODEL)),
        'x_sample': nrm(ks[1], (DEC_BATCH, DEC_SEQ, D_MODEL)),
        'cache_k': nrm(ks[2], cache_shape),
        'cache_v': nrm(ks[3], cache_shape),
        'c': nrm(ks[4], (DEC_BATCH, D_MODEL)),
        'c_ctx': nrm(ks[5], (D_MODEL,)),
        'norm_mix': 1.0 + nrm(ks[6], (DEPTH, D_MODEL), 0.1),
        'norm_ffn': 1.0 + nrm(ks[7], (DEPTH, D_MODEL), 0.1),
        'ada_w': nrm(ks[8], (DEPTH, D_MODEL, 6 * D_MODEL), 0.5 * D_MODEL ** -0.5),
        'ada_b': nrm(ks[9], (DEPTH, 6 * D_MODEL), 0.02),
        'attn_w_qkv': nrm(ks[10], (N_ATTN_LAYERS, D_MODEL, QKV_DIM), D_MODEL ** -0.5),
        'attn_q_norm': 1.0 + nrm(ks[11], (N_ATTN_LAYERS, HEAD_DIM), 0.1),
        'attn_k_norm': 1.0 + nrm(ks[12], (N_ATTN_LAYERS, HEAD_DIM), 0.1),
        'attn_sink': nrm(ks[13], (N_ATTN_LAYERS, N_HEADS)),
        'attn_w_o': nrm(ks[14], (N_ATTN_LAYERS, N_HEADS * HEAD_DIM, D_MODEL), (N_HEADS * HEAD_DIM) ** -0.5),
        'pool_w': nrm(ks[15], (N_POOL_LAYERS, N_POOL_GROUPS, POOL_GROUP_DIM, POOL_GROUP_DIM), POOL_GROUP_DIM ** -0.5),
        'pool_scale': 1.0 + nrm(ks[16], (N_POOL_LAYERS, D_MODEL), 0.1),
        'fnet_w': nrm(ks[17], (N_FFT_LAYERS, D_MODEL, D_MODEL), D_MODEL ** -0.5),
        'ffn_w_in': nrm(ks[18], (DEPTH, D_MODEL, 2 * D_FF), D_MODEL ** -0.5),
        'ffn_conv_w': nrm(ks[19], (DEPTH, CONV_W, 2 * D_FF), CONV_W ** -0.5),
        'ffn_conv_b': nrm(ks[20], (DEPTH, 2 * D_FF), 0.02),
        'ffn_w_out': nrm(ks[21], (DEPTH, D_FF, D_MODEL), D_FF ** -0.5),
    }


def reference(x_prompt, x_sample, cache_k, cache_v, c, c_ctx, norm_mix, norm_ffn, ada_w, ada_b,
              attn_w_qkv, attn_q_norm, attn_k_norm, attn_sink, attn_w_o, pool_w, pool_scale, fnet_w,
              ffn_w_in, ffn_conv_w, ffn_conv_b, ffn_w_out):
    weights = (norm_mix, norm_ffn, ada_w, ada_b, attn_w_qkv, attn_q_norm, attn_k_norm, attn_sink,
               attn_w_o, pool_w, pool_scale, fnet_w, ffn_w_in, ffn_conv_w, ffn_conv_b, ffn_w_out)
    y_prompt, ctx_ks, ctx_vs = trunk(x_prompt, c_ctx, None, None, weights)
    new_cache_k = jnp.stack(ctx_ks, axis=1)
    new_cache_v = jnp.stack(ctx_vs, axis=1)
    y_sample, _, _ = trunk(x_sample, c, cache_k, cache_v, weights)
    return (y_prompt, y_sample, new_cache_k, new_cache_v)
```

```python
import functools
import math

import jax
import jax.numpy as jnp
import numpy as np
from jax import lax
from jax.experimental import pallas as pl
from jax.experimental.pallas import tpu as pltpu

D_MODEL = 1024
DEPTH = 4
GRID_W = 64
N_MIXERS = 3
N_HEADS = 16
N_KV_HEADS = 4
HEAD_DIM = D_MODEL // N_HEADS
Q_PER_KV = N_HEADS // N_KV_HEADS
KV_DIM = N_KV_HEADS * HEAD_DIM
QKV_DIM = (N_HEADS + 2 * N_KV_HEADS) * HEAD_DIM
WINDOW = 128
BLOCK = 128
ROPE_THETA = 10000.0
POOL_WINDOWS = (2, 4, 8, 16)
N_GROUPS = 4
GROUP_DIM = D_MODEL // N_GROUPS
D_FF = ((8 * D_MODEL // 3 + 127) // 128) * 128
EPS = 1e-6
NEG_INF = -1e30

V7X_LANES = 128
V7X_SUBLANES = 8
V7X_BF16_ROWS = 16
V7X_MXU_DIM = 256
V7X_VMEM_BYTES = 64 * 1024 * 1024

F32 = jnp.float32
BF16 = jnp.bfloat16


def _vmem_limit(nbytes):
    return int(min(V7X_VMEM_BYTES - (4 << 20), nbytes * 3 // 2 + (4 << 20)))


def _silu(x):
    return x * (1.0 / (1.0 + jnp.exp(-x)))


def _split_bf16(a):
    hi = a.astype(BF16)
    lo = (a - hi.astype(F32)).astype(BF16)
    return hi, lo


def _norm_mod(x, gs, sh):
    ms = jnp.mean(x * x, axis=-1, keepdims=True)
    return x * lax.rsqrt(ms + EPS) * gs + sh


def _dot(a, b):
    return jnp.dot(a, b, preferred_element_type=F32)


ADA_ROWS = 16
ADA_TN = 1536


def _ada_kernel(c_ref, w_ref, b_ref, o_ref):
    s_hi, s_lo = _split_bf16(_silu(c_ref[...]))
    w_hi, w_lo = _split_bf16(w_ref[0])
    acc = _dot(s_hi, w_hi) + (_dot(s_lo, w_hi) + _dot(s_hi, w_lo))
    o_ref[0] = acc + b_ref[0]


def _ada_mods(cond, ada_w, ada_b):
    n_out = 6 * D_MODEL
    blk = D_MODEL * ADA_TN * 4
    return pl.pallas_call(
        _ada_kernel,
        grid=(DEPTH, n_out // ADA_TN),
        in_specs=[
            pl.BlockSpec((ADA_ROWS, D_MODEL), lambda i, j: (0, 0)),
            pl.BlockSpec((1, D_MODEL, ADA_TN), lambda i, j: (i, 0, j)),
            pl.BlockSpec((1, 1, ADA_TN), lambda i, j: (i, 0, j)),
        ],
        out_specs=pl.BlockSpec((1, ADA_ROWS, ADA_TN), lambda i, j: (i, 0, j)),
        out_shape=jax.ShapeDtypeStruct((DEPTH, ADA_ROWS, n_out), F32),
        compiler_params=pltpu.CompilerParams(
            dimension_semantics=("arbitrary", "arbitrary"),
            vmem_limit_bytes=_vmem_limit(2 * blk + 2 * blk),
        ),
        name="ada_mods",
    )(cond, ada_w, ada_b.reshape(DEPTH, 1, n_out))


FFN_CN = 256
FFN_HALO = 8
FFN_RC = 128


def _shift_rows(u, edge_row, down):
    n = u.shape[0]
    rid = lax.broadcasted_iota(jnp.int32, (V7X_SUBLANES, u.shape[1]), 0)
    if down:
        r = pltpu.roll(u, 1, 0)
        first = jnp.where(rid == 0, edge_row, r[:V7X_SUBLANES])
        return jnp.concatenate([first, r[V7X_SUBLANES:]], axis=0)
    r = pltpu.roll(u, n - 1, 0)
    last = jnp.where(rid == V7X_SUBLANES - 1, edge_row, r[n - V7X_SUBLANES:])
    return jnp.concatenate([r[:n - V7X_SUBLANES], last], axis=0)


def _ffn_kernel(xm_ref, xp_ref, xn_ref, g_ref, sh_ref, sc_ref, gate_ref, win_ref, cw_ref, cb_ref,
                wout_ref, o_ref, h_ref, a_ref, *, tm, nt):
    t = pl.program_id(1)
    gs = g_ref[...] * (1.0 + sc_ref[0])
    sh = sh_ref[0]

    def norm_rows(i, carry):
        r = pl.multiple_of(i * FFN_RC, FFN_RC)
        h_ref[pl.ds(r, FFN_RC), :] = _norm_mod(xm_ref[0, pl.ds(r, FFN_RC), :], gs, sh).astype(BF16)
        return carry

    lax.fori_loop(0, tm // FFN_RC, norm_rows, 0)
    hp = _norm_mod(xp_ref[0], gs, sh) * (t > 0).astype(F32)
    hn = _norm_mod(xn_ref[0], gs, sh) * (t < nt - 1).astype(F32)
    h_ref[tm:tm + 2 * FFN_HALO, :] = jnp.concatenate([hp, hn], axis=0).astype(BF16)

    def conv(u, c0):
        um = u[:tm]
        prev_row = u[tm + FFN_HALO - 1:tm + FFN_HALO]
        next_row = u[tm + FFN_HALO:tm + FFN_HALO + 1]
        w = cw_ref[:, c0:c0 + FFN_CN]
        return (_shift_rows(um, prev_row, True) * w[0:1] + um * w[1:2]
                + _shift_rows(um, next_row, False) * w[2:3] + cb_ref[:, c0:c0 + FFN_CN])

    for j in range(D_FF // FFN_CN):
        c0 = j * FFN_CN
        hs = h_ref[...]
        gate = conv(_dot(hs, win_ref[:, c0:c0 + FFN_CN]), c0)
        val = conv(_dot(hs, win_ref[:, D_FF + c0:D_FF + c0 + FFN_CN]), D_FF + c0)
        a_ref[:, c0:c0 + FFN_CN] = (_silu(gate) * val).astype(BF16)

    o_ref[0] = xm_ref[0] + gate_ref[0] * _dot(a_ref[...], wout_ref[...])


def _ffn_layer(x, g, sh, sc, gate, w_in, conv_w, conv_b, w_out, tm):
    B, L, D = x.shape
    nt = L // tm
    bm = sh.shape[0]
    hb = tm // FFN_HALO
    mod_spec = pl.BlockSpec((1, 1, D), (lambda b, t: (b, 0, 0)) if bm > 1 else (lambda b, t: (0, 0, 0)))
    const = lambda b, t: (0, 0)
    est = (2 * D * 2 * D_FF * 2 + 2 * D_FF * D * 2 + 4 * tm * D * 4
           + (tm + 16) * D * 2 + tm * D_FF * 2 + 6 * (tm + 16) * FFN_CN * 4)
    return pl.pallas_call(
        functools.partial(_ffn_kernel, tm=tm, nt=nt),
        grid=(B, nt),
        in_specs=[
            pl.BlockSpec((1, tm, D), lambda b, t: (b, t, 0)),
            pl.BlockSpec((1, FFN_HALO, D), lambda b, t: (b, jnp.maximum(t * hb - 1, 0), 0)),
            pl.BlockSpec((1, FFN_HALO, D), lambda b, t: (b, jnp.minimum((t + 1) * hb, L // FFN_HALO - 1), 0)),
            pl.BlockSpec((1, D), const),
            mod_spec, mod_spec, mod_spec,
            pl.BlockSpec((D, 2 * D_FF), const),
            pl.BlockSpec((3, 2 * D_FF), const),
            pl.BlockSpec((1, 2 * D_FF), const),
            pl.BlockSpec((D_FF, D), const),
        ],
        out_specs=pl.BlockSpec((1, tm, D), lambda b, t: (b, t, 0)),
        out_shape=jax.ShapeDtypeStruct((B, L, D), F32),
        scratch_shapes=[
            pltpu.VMEM((tm + 2 * FFN_HALO, D), BF16),
            pltpu.VMEM((tm, D_FF), BF16),
        ],
        compiler_params=pltpu.CompilerParams(
            dimension_semantics=("arbitrary", "arbitrary"),
            vmem_limit_bytes=_vmem_limit(est),
        ),
        name="conv_ffn",
    )(x, x, x, g.reshape(1, D), sh, sc, gate, w_in, conv_w, conv_b.reshape(1, 2 * D_FF), w_out)


def _mod_spec(bm):
    return pl.BlockSpec((1, 1, D_MODEL), (lambda b, t: (b, 0, 0)) if bm > 1 else (lambda b, t: (0, 0, 0)))


_CONST2 = lambda b, t: (0, 0)


def _proj_res_kernel(x_ref, a_ref, w_ref, gate_ref, o_ref):
    o_ref[0] = x_ref[0] + gate_ref[0] * _dot(a_ref[0], w_ref[...])


def _proj_res(x, a, w, gate, tm):
    B, L, D = x.shape
    kdim = a.shape[-1]
    est = 2 * kdim * D * 2 + 4 * tm * D * 4 + 2 * tm * kdim * 2 + tm * D * 4
    return pl.pallas_call(
        _proj_res_kernel,
        grid=(B, L // tm),
        in_specs=[
            pl.BlockSpec((1, tm, D), lambda b, t: (b, t, 0)),
            pl.BlockSpec((1, tm, kdim), lambda b, t: (b, t, 0)),
            pl.BlockSpec((kdim, D), _CONST2),
            _mod_spec(gate.shape[0]),
        ],
        out_specs=pl.BlockSpec((1, tm, D), lambda b, t: (b, t, 0)),
        out_shape=jax.ShapeDtypeStruct((B, L, D), F32),
        compiler_params=pltpu.CompilerParams(
            dimension_semantics=("arbitrary", "arbitrary"), vmem_limit_bytes=_vmem_limit(est)),
        name="proj_res",
    )(x, a, w, gate)


def _head_rms(y, e, et, width):
    s_hi, s_lo = _split_bf16(y * y)
    ss = _dot(s_hi, e[:width]) + _dot(s_lo, e[:width])
    r_hi, r_lo = _split_bf16(lax.rsqrt(ss * (1.0 / HEAD_DIM) + EPS))
    return _dot(r_hi, et[:, :width]) + _dot(r_lo, et[:, :width])


def _rope(y, cos, sin_signed, first):
    outs = []
    for c0 in range(0, y.shape[1], V7X_LANES):
        ys = y[:, c0:c0 + V7X_LANES]
        partner = jnp.where(first, pltpu.roll(ys, V7X_LANES - 16, 1), pltpu.roll(ys, 16, 1))
        outs.append(ys * cos + partner * sin_signed)
    return jnp.concatenate(outs, axis=1)


def _qkv_kernel(*refs, rope, keep_f32):
    x_ref, g_ref, sh_ref, sc_ref, w_ref, qn_ref, kn_ref, e_ref, et_ref = refs[:9]
    refs = refs[9:]
    if rope:
        cos_ref, sin_ref = refs[:2]
        refs = refs[2:]
    q_ref, k_ref, v_ref = refs[:3]
    gs = g_ref[...] * (1.0 + sc_ref[0])
    h = _norm_mod(x_ref[0], gs, sh_ref[0]).astype(BF16)
    qkv = _dot(h, w_ref[...])
    q = qkv[:, :D_MODEL]
    k = qkv[:, D_MODEL:D_MODEL + KV_DIM]
    v = qkv[:, D_MODEL + KV_DIM:]
    e = e_ref[...]
    et = et_ref[...]
    q = q * _head_rms(q, e, et, D_MODEL) * (qn_ref[...] * HEAD_DIM ** -0.5)
    k = k * _head_rms(k, e, et, KV_DIM) * kn_ref[...]
    if keep_f32:
        refs[3][0] = k
        refs[4][0] = v
    if rope:
        lane = lax.broadcasted_iota(jnp.int32, (x_ref.shape[1], V7X_LANES), 1)
        first = (lane % 32) < 16
        q = _rope(q, cos_ref[...], sin_ref[...], first)
        k = _rope(k, cos_ref[...], sin_ref[...], first)
    q_ref[0] = q.astype(BF16)
    k_ref[0] = k.astype(BF16)
    v_ref[0] = v.astype(BF16)


def _head_selectors():
    col = np.arange(D_MODEL)[:, None] // HEAD_DIM
    e = (col == np.arange(V7X_LANES)[None, :]).astype(np.float32)
    return jnp.asarray(e, BF16), jnp.asarray(e.T, BF16)


def _rope_tables(L):
    half = HEAD_DIM // 2
    rows = jnp.repeat(jnp.arange(L // GRID_W), GRID_W)
    cols = jnp.tile(jnp.arange(GRID_W), L // GRID_W)
    inv_freq = 1.0 / (ROPE_THETA ** (jnp.arange(0, half, 2, dtype=F32) / half))
    ar = rows.astype(F32)[:, None] * inv_freq
    ac = cols.astype(F32)[:, None] * inv_freq
    cos = jnp.concatenate([jnp.cos(ar), jnp.cos(ar), jnp.cos(ac), jnp.cos(ac)], axis=1)
    sin = jnp.concatenate([-jnp.sin(ar), jnp.sin(ar), -jnp.sin(ac), jnp.sin(ac)], axis=1)
    return jnp.tile(cos, (1, 2)), jnp.tile(sin, (1, 2))


def _qkv_layer(x, g, sh, sc, w_qkv, q_norm, k_norm, tm, rope):
    B, L, D = x.shape
    keep_f32 = not rope
    e, et = _head_selectors()
    qn = jnp.tile(q_norm, N_HEADS).reshape(1, D)
    kn = jnp.tile(k_norm, N_KV_HEADS).reshape(1, KV_DIM)
    ms = _mod_spec(sh.shape[0])
    in_specs = [
        pl.BlockSpec((1, tm, D), lambda b, t: (b, t, 0)),
        pl.BlockSpec((1, D), _CONST2), ms, ms,
        pl.BlockSpec((D, QKV_DIM), _CONST2),
        pl.BlockSpec((1, D), _CONST2),
        pl.BlockSpec((1, KV_DIM), _CONST2),
        pl.BlockSpec((D, V7X_LANES), _CONST2),
        pl.BlockSpec((V7X_LANES, D), _CONST2),
    ]
    args = [x, g.reshape(1, D), sh, sc, w_qkv, qn, kn, e, et]
    if rope:
        cos, sin = _rope_tables(L)
        in_specs += [pl.BlockSpec((tm, V7X_LANES), lambda b, t: (t, 0))] * 2
        args += [cos, sin]
    tok = lambda w: pl.BlockSpec((1, tm, w), lambda b, t: (b, t, 0))
    out_specs = [tok(D), tok(KV_DIM), tok(KV_DIM)]
    out_shape = [jax.ShapeDtypeStruct((B, L, D), BF16), jax.ShapeDtypeStruct((B, L, KV_DIM), BF16),
                 jax.ShapeDtypeStruct((B, L, KV_DIM), BF16)]
    if keep_f32:
        out_specs += [tok(KV_DIM), tok(KV_DIM)]
        out_shape += [jax.ShapeDtypeStruct((B, L, KV_DIM), F32)] * 2
    est = 2 * D * QKV_DIM * 2 + 2 * tm * D * 4 + 10 * tm * QKV_DIM * 4
    return pl.pallas_call(
        functools.partial(_qkv_kernel, rope=rope, keep_f32=keep_f32),
        grid=(B, L // tm),
        in_specs=in_specs,
        out_specs=out_specs,
        out_shape=out_shape,
        compiler_params=pltpu.CompilerParams(
            dimension_semantics=("arbitrary", "arbitrary"), vmem_limit_bytes=_vmem_limit(est)),
        name="qkv_rope" if rope else "qkv",
    )(*args)


def _nt_dot(a, b):
    return lax.dot_general(a, b, (((1,), (1,)), ((), ())), preferred_element_type=F32)


def _attend(q, parts, sink_ref, o_ref):
    for kh in range(N_KV_HEADS):
        heads = [kh * Q_PER_KV + g for g in range(Q_PER_KV)]
        qs = jnp.concatenate([q[:, h * HEAD_DIM:(h + 1) * HEAD_DIM] for h in heads], axis=0)
        sink = jnp.concatenate([jnp.full((BLOCK, 1), sink_ref[h], F32) for h in heads], axis=0)
        lo, hi = kh * HEAD_DIM, (kh + 1) * HEAD_DIM
        scores = []
        m = sink
        for k, _, mask in parts:
            s = _nt_dot(qs, k[:, lo:hi])
            if mask is not None:
                s = jnp.where(mask, s, NEG_INF)
            scores.append(s)
            m = jnp.maximum(m, jnp.max(s, axis=-1, keepdims=True))
        denom = jnp.exp(sink - m)
        out = jnp.zeros((Q_PER_KV * BLOCK, HEAD_DIM), F32)
        for s, (_, v, _) in zip(scores, parts):
            p = jnp.exp(s - m)
            denom = denom + jnp.sum(p, axis=-1, keepdims=True)
            out = out + _dot(p.astype(BF16), v[:, lo:hi])
        out = (out / denom).astype(BF16)
        for g, h in enumerate(heads):
            o_ref[0, :, h * HEAD_DIM:(h + 1) * HEAD_DIM] = out[g * BLOCK:(g + 1) * BLOCK]


def _attn_latent_kernel(sink_ref, q_ref, kp_ref, kc_ref, kn_ref, vp_ref, vc_ref, vn_ref, ck_ref, cv_ref,
                        o_ref, *, nqb):
    n = pl.program_id(1)
    kw = jnp.concatenate([kp_ref[0], kc_ref[0], kn_ref[0]], axis=0)
    vw = jnp.concatenate([vp_ref[0], vc_ref[0], vn_ref[0]], axis=0)
    shape = (Q_PER_KV * BLOCK, 3 * BLOCK)
    qi = lax.broadcasted_iota(jnp.int32, shape, 0) % BLOCK
    sj = lax.broadcasted_iota(jnp.int32, shape, 1)
    band = jnp.abs(sj - BLOCK - qi) <= WINDOW
    lo = jnp.where(n > 0, 0, BLOCK)
    hi = jnp.where(n < nqb - 1, 3 * BLOCK, 2 * BLOCK)
    mask = band & (sj >= lo) & (sj < hi)
    _attend(q_ref[0], [(kw, vw, mask), (ck_ref[0], cv_ref[0], None)], sink_ref, o_ref)


def _attn_context_kernel(sink_ref, q_ref, k_ref, v_ref, o_ref):
    _attend(q_ref[0], [(k_ref[0], v_ref[0], None)], sink_ref, o_ref)


def _attention(q, k, v, sink, ctx_k=None, ctx_v=None):
    B, L, D = q.shape
    nqb = L // BLOCK
    qspec = pl.BlockSpec((1, BLOCK, D), lambda b, n: (b, n, 0))
    smem = pl.BlockSpec(memory_space=pltpu.SMEM)
    if ctx_k is None:
        kv = pl.BlockSpec((1, L, KV_DIM), lambda b, n: (b, 0, 0))
        body, in_specs, args = _attn_context_kernel, [smem, qspec, kv, kv], (sink, q, k, v)
    else:
        P = ctx_k.shape[1]
        prv = pl.BlockSpec((1, BLOCK, KV_DIM), lambda b, n: (b, jnp.maximum(n - 1, 0), 0))
        cur = pl.BlockSpec((1, BLOCK, KV_DIM), lambda b, n: (b, n, 0))
        nxt = pl.BlockSpec((1, BLOCK, KV_DIM), lambda b, n: (b, jnp.minimum(n + 1, nqb - 1), 0))
        ctx = pl.BlockSpec((1, P, KV_DIM), lambda b, n: (b, 0, 0))
        body = functools.partial(_attn_latent_kernel, nqb=nqb)
        in_specs = [smem, qspec, prv, cur, nxt, prv, cur, nxt, ctx, ctx]
        args = (sink, q, k, k, k, v, v, v, ctx_k, ctx_v)
    return pl.pallas_call(
        body,
        grid=(B, nqb),
        in_specs=in_specs,
        out_specs=qspec,
        out_shape=jax.ShapeDtypeStruct((B, L, D), BF16),
        compiler_params=pltpu.CompilerParams(
            dimension_semantics=("arbitrary", "arbitrary"), vmem_limit_bytes=_vmem_limit(24 << 20)),
        name="attn_context" if ctx_k is None else "attn_latent",
    )(*args)


POOL_TM = 256
POOL_HALO = 8


def _pool_kernel(xm_ref, xp_ref, xn_ref, g_ref, sh_ref, sc_ref, gate_ref, band_ref, pw_ref, ps_ref, o_ref,
                 *, nt, seq_len):
    t = pl.program_id(1)
    gs = g_ref[...] * (1.0 + sc_ref[0])
    sh = sh_ref[0]
    xm = xm_ref[0]
    hm = _norm_mod(xm, gs, sh)
    hp = _norm_mod(xp_ref[0], gs, sh) * (t > 0).astype(F32)
    hn = _norm_mod(xn_ref[0], gs, sh) * (t < nt - 1).astype(F32)
    h_hi, h_lo = _split_bf16(jnp.concatenate([hp, hm, hn], axis=0))
    tpos = t * POOL_TM + lax.broadcasted_iota(jnp.int32, (POOL_TM, GROUP_DIM), 0)
    gate = gate_ref[0]
    for g, win in enumerate(POOL_WINDOWS):
        back = win // 2
        fwd = win - back - 1
        cs = slice(g * GROUP_DIM, (g + 1) * GROUP_DIM)
        band = band_ref[g]
        wsum = _dot(band, h_hi[:, cs]) + _dot(band, h_lo[:, cs])
        cnt = jnp.minimum(tpos + fwd + 1, seq_len) - jnp.maximum(tpos - back, 0)
        pooled = wsum / cnt.astype(F32) - hm[:, cs]
        mixed = _dot(pooled.astype(BF16), pw_ref[g])
        o_ref[0, :, cs] = xm[:, cs] + gate[:, cs] * (mixed * ps_ref[:, cs])


def _pool_bands():
    r = np.arange(POOL_TM)[:, None] + POOL_HALO
    c = np.arange(POOL_TM + 2 * POOL_HALO)[None, :]
    bands = [((c >= r - w // 2) & (c <= r + (w - w // 2 - 1))) for w in POOL_WINDOWS]
    return jnp.asarray(np.stack(bands).astype(np.float32), BF16)


def _pool_layer(x, g, sh, sc, gate, pool_w, pool_scale):
    B, L, D = x.shape
    tm = POOL_TM
    nt = L // tm
    hb = tm // POOL_HALO
    ms = _mod_spec(sh.shape[0])
    ext = tm + 2 * POOL_HALO
    return pl.pallas_call(
        functools.partial(_pool_kernel, nt=nt, seq_len=L),
        grid=(B, nt),
        in_specs=[
            pl.BlockSpec((1, tm, D), lambda b, t: (b, t, 0)),
            pl.BlockSpec((1, POOL_HALO, D), lambda b, t: (b, jnp.maximum(t * hb - 1, 0), 0)),
            pl.BlockSpec((1, POOL_HALO, D), lambda b, t: (b, jnp.minimum((t + 1) * hb, L // POOL_HALO - 1), 0)),
            pl.BlockSpec((1, D), _CONST2), ms, ms, ms,
            pl.BlockSpec((N_GROUPS, tm, ext), lambda b, t: (0, 0, 0)),
            pl.BlockSpec((N_GROUPS, GROUP_DIM, GROUP_DIM), lambda b, t: (0, 0, 0)),
            pl.BlockSpec((1, D), _CONST2),
        ],
        out_specs=pl.BlockSpec((1, tm, D), lambda b, t: (b, t, 0)),
        out_shape=jax.ShapeDtypeStruct((B, L, D), F32),
        compiler_params=pltpu.CompilerParams(
            dimension_semantics=("arbitrary", "arbitrary"), vmem_limit_bytes=_vmem_limit(16 << 20)),
        name="pool_mix",
    )(x, x, x, g.reshape(1, D), sh, sc, gate, _pool_bands(), pool_w, pool_scale.reshape(1, D))


def _dft_tables(n):
    k = np.arange(n)
    ang = 2.0 * np.pi * ((k[:, None] * k[None, :]) % n) / n
    return np.cos(ang), np.sin(ang)


def _chan_dft_kernel(x_ref, g_ref, sh_ref, sc_ref, cc_ref, sc_tab_ref, y_ref):
    gs = g_ref[...] * (1.0 + sc_ref[0])
    h = _norm_mod(x_ref[0], gs, sh_ref[0]).astype(BF16)
    for g in range(N_GROUPS):
        cs = slice(g * GROUP_DIM, (g + 1) * GROUP_DIM)
        y_ref[0, 0, :, cs] = _dot(h[:, cs], cc_ref[...]).astype(BF16)
        y_ref[0, 1, :, cs] = _dot(h[:, cs], sc_tab_ref[...]).astype(BF16)


def _chan_dft(x, g, sh, sc, tm):
    B, L, D = x.shape
    cc, sn = _dft_tables(GROUP_DIM)
    ms = _mod_spec(sh.shape[0])
    tab = pl.BlockSpec((GROUP_DIM, GROUP_DIM), _CONST2)
    return pl.pallas_call(
        _chan_dft_kernel,
        grid=(B, L // tm),
        in_specs=[pl.BlockSpec((1, tm, D), lambda b, t: (b, t, 0)), pl.BlockSpec((1, D), _CONST2), ms, ms, tab, tab],
        out_specs=pl.BlockSpec((1, 2, tm, D), lambda b, t: (b, 0, t, 0)),
        out_shape=jax.ShapeDtypeStruct((B, 2, L, D), BF16),
        compiler_params=pltpu.CompilerParams(
            dimension_semantics=("arbitrary", "arbitrary"), vmem_limit_bytes=_vmem_limit(12 * tm * D * 4)),
        name="chan_dft",
    )(x, g.reshape(1, D), sh, sc, jnp.asarray(cc, BF16), jnp.asarray(sn, BF16))


def _pos_dft_kernel(w_ref, y_ref, o_ref, *, scale):
    o_ref[0] = (_dot(w_ref[...], y_ref[0]) * scale).astype(BF16)


def _pos_dft(y, L, tmo, tn):
    B, _, D = y.shape
    cl, sl = _dft_tables(L)
    w = jnp.asarray(np.concatenate([cl, -sl], axis=1), BF16)
    scale = 1.0 / math.sqrt(L * GROUP_DIM)
    est = 2 * tmo * 2 * L * 2 + 2 * 2 * L * tn * 2 + 3 * tmo * tn * 4
    return pl.pallas_call(
        functools.partial(_pos_dft_kernel, scale=scale),
        grid=(B, D // tn, L // tmo),
        in_specs=[
            pl.BlockSpec((tmo, 2 * L), lambda b, j, i: (i, 0)),
            pl.BlockSpec((1, 2 * L, tn), lambda b, j, i: (b, 0, j)),
        ],
        out_specs=pl.BlockSpec((1, tmo, tn), lambda b, j, i: (b, i, j)),
        out_shape=jax.ShapeDtypeStruct((B, L, D), BF16),
        compiler_params=pltpu.CompilerParams(
            dimension_semantics=("arbitrary", "arbitrary", "arbitrary"), vmem_limit_bytes=_vmem_limit(est)),
        name="pos_dft",
    )(w, y)


def _trunk(x, mods, ctx_k, ctx_v, wts, tm):
    (norm_mix, norm_ffn, w_qkv, q_norm, k_norm, sink, w_o, pool_w, pool_scale, fnet_w,
     w_in, conv_w, conv_b, w_out) = wts
    B, L, D = x.shape
    latent = ctx_k is not None
    ks, vs = [], []
    for i in range(DEPTH):
        kind, j = i % N_MIXERS, i // N_MIXERS
        sh1, sc1, g1, sh2, sc2, g2 = mods[i]
        if kind == 0:
            outs = _qkv_layer(x, norm_mix[i], sh1, sc1, w_qkv[j], q_norm[j], k_norm[j], tm, rope=latent)
            q, k, v = outs[:3]
            if latent:
                o = _attention(q, k, v, sink[j], ctx_k[j], ctx_v[j])
            else:
                o = _attention(q, k, v, sink[j])
                ks.append(outs[3].reshape(B, L, N_KV_HEADS, HEAD_DIM))
                vs.append(outs[4].reshape(B, L, N_KV_HEADS, HEAD_DIM))
            x = _proj_res(x, o, w_o[j], g1, tm)
        elif kind == 1:
            x = _pool_layer(x, norm_mix[i], sh1, sc1, g1, pool_w[j], pool_scale[j])
        else:
            y = _chan_dft(x, norm_mix[i], sh1, sc1, tm).reshape(B, 2 * L, D)
            f = _pos_dft(y, L, min(L, 512), 512 if L > 512 else D)
            x = _proj_res(x, f, fnet_w[j], g1, tm)
        x = _ffn_layer(x, norm_ffn[i], sh2, sc2, g2, w_in[i], conv_w[i], conv_b[i], w_out[i], tm)
    return x, ks, vs


def kernel(x_prompt, x_sample, cache_k, cache_v, c, c_ctx, norm_mix, norm_ffn, ada_w, ada_b, attn_w_qkv,
           attn_q_norm, attn_k_norm, attn_sink, attn_w_o, pool_w, pool_scale, fnet_w, ffn_w_in, ffn_conv_w,
           ffn_conv_b, ffn_w_out):
    nb = c.shape[0]
    cond = jnp.concatenate([c, c_ctx[None], jnp.zeros((ADA_ROWS - nb - 1, D_MODEL), F32)], axis=0)
    mods = _ada_mods(cond, ada_w, ada_b).reshape(DEPTH, ADA_ROWS, 6, 1, D_MODEL)
    mods_sample = [[mods[i, :nb, m] for m in range(6)] for i in range(DEPTH)]
    mods_prompt = [[mods[i, nb:nb + 1, m] for m in range(6)] for i in range(DEPTH)]
    wts = (norm_mix, norm_ffn, attn_w_qkv.astype(BF16), attn_q_norm, attn_k_norm, attn_sink,
           attn_w_o.astype(BF16), pool_w.astype(BF16), pool_scale, fnet_w.astype(BF16),
           ffn_w_in.astype(BF16), ffn_conv_w, ffn_conv_b, ffn_w_out.astype(BF16))
    y_prompt, ks, vs = _trunk(x_prompt, mods_prompt, None, None, wts, tm=256)
    past = cache_k.shape[2]
    ctx_k = [cache_k[:, j].reshape(nb, past, KV_DIM).astype(BF16) for j in range(cache_k.shape[1])]
    ctx_v = [cache_v[:, j].reshape(nb, past, KV_DIM).astype(BF16) for j in range(cache_v.shape[1])]
    y_sample, _, _ = _trunk(x_sample, mods_sample, ctx_k, ctx_v, wts, tm=512)
    return (y_prompt, y_sample, jnp.stack(ks, axis=1), jnp.stack(vs, axis=1))
```

```python
import functools
import math

import jax
import jax.numpy as jnp
import numpy as np
from jax import lax
from jax.experimental import pallas as pl
from jax.experimental.pallas import tpu as pltpu

D_MODEL = 1024
DEPTH = 4
GRID_W = 64
N_MIXERS = 3
N_HEADS = 16
N_KV_HEADS = 4
HEAD_DIM = D_MODEL // N_HEADS
Q_PER_KV = N_HEADS // N_KV_HEADS
KV_DIM = N_KV_HEADS * HEAD_DIM
QKV_DIM = (N_HEADS + 2 * N_KV_HEADS) * HEAD_DIM
WINDOW = 128
BLOCK = 128
ROPE_THETA = 10000.0
POOL_WINDOWS = (2, 4, 8, 16)
N_GROUPS = 4
GROUP_DIM = D_MODEL // N_GROUPS
D_FF = ((8 * D_MODEL // 3 + 127) // 128) * 128
EPS = 1e-6
NEG_INF = -1e30
LOG2E = math.log2(math.e)

V7X_LANES = 128
V7X_SUBLANES = 8
V7X_BF16_ROWS = 16
V7X_MXU_DIM = 256
V7X_VMEM_BYTES = 64 * 1024 * 1024

F32 = jnp.float32
BF16 = jnp.bfloat16


def _vmem_limit(nbytes):
    return int(min(V7X_VMEM_BYTES - (4 << 20), nbytes * 3 // 2 + (4 << 20)))


def _silu(x):
    return x * (1.0 / (1.0 + jnp.exp(-x)))


def _split_bf16(a):
    hi = a.astype(BF16)
    lo = (a - hi.astype(F32)).astype(BF16)
    return hi, lo


def _norm_mod(x, gs, sh):
    ms = jnp.mean(x * x, axis=-1, keepdims=True)
    return x * lax.rsqrt(ms + EPS) * gs + sh


def _dot(a, b):
    return jnp.dot(a, b, preferred_element_type=F32)


ADA_ROWS = 16
ADA_TN = 1536


def _ada_kernel(c_ref, w_ref, b_ref, o_ref):
    s_hi, s_lo = _split_bf16(_silu(c_ref[...]))
    w_hi, w_lo = _split_bf16(w_ref[0])
    acc = _dot(s_hi, w_hi) + (_dot(s_lo, w_hi) + _dot(s_hi, w_lo))
    o_ref[0] = acc + b_ref[0]


def _ada_mods(cond, ada_w, ada_b):
    n_out = 6 * D_MODEL
    blk = D_MODEL * ADA_TN * 4
    return pl.pallas_call(
        _ada_kernel,
        grid=(DEPTH, n_out // ADA_TN),
        in_specs=[
            pl.BlockSpec((ADA_ROWS, D_MODEL), lambda i, j: (0, 0)),
            pl.BlockSpec((1, D_MODEL, ADA_TN), lambda i, j: (i, 0, j)),
            pl.BlockSpec((1, 1, ADA_TN), lambda i, j: (i, 0, j)),
        ],
        out_specs=pl.BlockSpec((1, ADA_ROWS, ADA_TN), lambda i, j: (i, 0, j)),
        out_shape=jax.ShapeDtypeStruct((DEPTH, ADA_ROWS, n_out), F32),
        compiler_params=pltpu.CompilerParams(
            dimension_semantics=("arbitrary", "arbitrary"),
            vmem_limit_bytes=_vmem_limit(2 * blk + 2 * blk),
        ),
        name="ada_mods",
    )(cond, ada_w, ada_b.reshape(DEPTH, 1, n_out))


FFN_CN = 256
FFN_HALO = 8
FFN_RC = 128


def _shift_rows(u, edge_row, down):
    n = u.shape[0]
    rid = lax.broadcasted_iota(jnp.int32, (V7X_SUBLANES, u.shape[1]), 0)
    if down:
        r = pltpu.roll(u, 1, 0)
        first = jnp.where(rid == 0, edge_row, r[:V7X_SUBLANES])
        return jnp.concatenate([first, r[V7X_SUBLANES:]], axis=0)
    r = pltpu.roll(u, n - 1, 0)
    last = jnp.where(rid == V7X_SUBLANES - 1, edge_row, r[n - V7X_SUBLANES:])
    return jnp.concatenate([r[:n - V7X_SUBLANES], last], axis=0)


def _ffn_kernel(xm_ref, xp_ref, xn_ref, g_ref, sh_ref, sc_ref, gate_ref, win_ref, cw_ref, cb_ref,
                wout_ref, o_ref, h_ref, a_ref, *, tm, nt):
    t = pl.program_id(1)
    gs = g_ref[...] * (1.0 + sc_ref[0])
    sh = sh_ref[0]

    for r in range(0, tm, FFN_RC):
        h_ref[r:r + FFN_RC, :] = _norm_mod(xm_ref[0, r:r + FFN_RC, :], gs, sh).astype(BF16)
    hp = _norm_mod(xp_ref[0], gs, sh) * (t > 0).astype(F32)
    hn = _norm_mod(xn_ref[0], gs, sh) * (t < nt - 1).astype(F32)
    h_ref[tm:tm + 2 * FFN_HALO, :] = jnp.concatenate([hp, hn], axis=0).astype(BF16)

    def conv(u, c0):
        um = u[:tm]
        prev_row = u[tm + FFN_HALO - 1:tm + FFN_HALO]
        next_row = u[tm + FFN_HALO:tm + FFN_HALO + 1]
        w = cw_ref[:, c0:c0 + FFN_CN]
        return (_shift_rows(um, prev_row, True) * w[0:1] + um * w[1:2]
                + _shift_rows(um, next_row, False) * w[2:3] + cb_ref[:, c0:c0 + FFN_CN])

    for j in range(D_FF // FFN_CN):
        c0 = j * FFN_CN
        hs = h_ref[...]
        gate = conv(_dot(hs, win_ref[:, c0:c0 + FFN_CN]), c0)
        val = conv(_dot(hs, win_ref[:, D_FF + c0:D_FF + c0 + FFN_CN]), D_FF + c0)
        a_ref[:, c0:c0 + FFN_CN] = (_silu(gate) * val).astype(BF16)

    o_ref[0] = xm_ref[0] + gate_ref[0] * _dot(a_ref[...], wout_ref[...])


def _ffn_layer(x, g, sh, sc, gate, w_in, conv_w, conv_b, w_out, tm):
    B, L, D = x.shape
    nt = L // tm
    bm = sh.shape[0]
    hb = tm // FFN_HALO
    mod_spec = pl.BlockSpec((1, 1, D), (lambda b, t: (b, 0, 0)) if bm > 1 else (lambda b, t: (0, 0, 0)))
    const = lambda b, t: (0, 0)
    est = (2 * D * 2 * D_FF * 2 + 2 * D_FF * D * 2 + 4 * tm * D * 4
           + (tm + 16) * D * 2 + tm * D_FF * 2 + 6 * (tm + 16) * FFN_CN * 4)
    return pl.pallas_call(
        functools.partial(_ffn_kernel, tm=tm, nt=nt),
        grid=(B, nt),
        in_specs=[
            pl.BlockSpec((1, tm, D), lambda b, t: (b, t, 0)),
            pl.BlockSpec((1, FFN_HALO, D), lambda b, t: (b, jnp.maximum(t * hb - 1, 0), 0)),
            pl.BlockSpec((1, FFN_HALO, D), lambda b, t: (b, jnp.minimum((t + 1) * hb, L // FFN_HALO - 1), 0)),
            pl.BlockSpec((1, D), const),
            mod_spec, mod_spec, mod_spec,
            pl.BlockSpec((D, 2 * D_FF), const),
            pl.BlockSpec((3, 2 * D_FF), const),
            pl.BlockSpec((1, 2 * D_FF), const),
            pl.BlockSpec((D_FF, D), const),
        ],
        out_specs=pl.BlockSpec((1, tm, D), lambda b, t: (b, t, 0)),
        out_shape=jax.ShapeDtypeStruct((B, L, D), F32),
        scratch_shapes=[
            pltpu.VMEM((tm + 2 * FFN_HALO, D), BF16),
            pltpu.VMEM((tm, D_FF), BF16),
        ],
        compiler_params=pltpu.CompilerParams(
            dimension_semantics=("arbitrary", "arbitrary"),
            vmem_limit_bytes=_vmem_limit(est),
        ),
        name="conv_ffn",
    )(x, x, x, g.reshape(1, D), sh, sc, gate, w_in, conv_w, conv_b.reshape(1, 2 * D_FF), w_out)


def _mod_spec(bm):
    return pl.BlockSpec((1, 1, D_MODEL), (lambda b, t: (b, 0, 0)) if bm > 1 else (lambda b, t: (0, 0, 0)))


_CONST2 = lambda b, t: (0, 0)


def _proj_res_kernel(x_ref, a_ref, w_ref, gate_ref, o_ref):
    o_ref[0] = x_ref[0] + gate_ref[0] * _dot(a_ref[0], w_ref[...])


def _proj_res(x, a, w, gate, tm):
    B, L, D = x.shape
    kdim = a.shape[-1]
    est = 2 * kdim * D * 2 + 4 * tm * D * 4 + 2 * tm * kdim * 2 + tm * D * 4
    return pl.pallas_call(
        _proj_res_kernel,
        grid=(B, L // tm),
        in_specs=[
            pl.BlockSpec((1, tm, D), lambda b, t: (b, t, 0)),
            pl.BlockSpec((1, tm, kdim), lambda b, t: (b, t, 0)),
            pl.BlockSpec((kdim, D), _CONST2),
            _mod_spec(gate.shape[0]),
        ],
        out_specs=pl.BlockSpec((1, tm, D), lambda b, t: (b, t, 0)),
        out_shape=jax.ShapeDtypeStruct((B, L, D), F32),
        compiler_params=pltpu.CompilerParams(
            dimension_semantics=("arbitrary", "arbitrary"), vmem_limit_bytes=_vmem_limit(est)),
        name="proj_res",
    )(x, a, w, gate)


def _head_rms(y, e, et, width):
    s_hi, s_lo = _split_bf16(y * y)
    ss = _dot(s_hi, e[:width]) + _dot(s_lo, e[:width])
    r_hi, r_lo = _split_bf16(lax.rsqrt(ss * (1.0 / HEAD_DIM) + EPS))
    return _dot(r_hi, et[:, :width]) + _dot(r_lo, et[:, :width])


def _rope(y, cos, sin_signed, first):
    outs = []
    for c0 in range(0, y.shape[1], V7X_LANES):
        ys = y[:, c0:c0 + V7X_LANES]
        partner = jnp.where(first, pltpu.roll(ys, V7X_LANES - 16, 1), pltpu.roll(ys, 16, 1))
        outs.append(ys * cos + partner * sin_signed)
    return jnp.concatenate(outs, axis=1)


def _qkv_kernel(*refs, rope, keep_f32):
    x_ref, g_ref, sh_ref, sc_ref, w_ref, qn_ref, kn_ref, e_ref, et_ref = refs[:9]
    refs = refs[9:]
    if rope:
        cos_ref, sin_ref, cost_ref, sint_ref = refs[:4]
        refs = refs[4:]
    q_ref, k_ref, v_ref = refs[:3]
    gs = g_ref[...] * (1.0 + sc_ref[0])
    h = _norm_mod(x_ref[0], gs, sh_ref[0]).astype(BF16)
    qkv = _dot(h, w_ref[...])
    k = qkv[:, D_MODEL:D_MODEL + KV_DIM]
    v = qkv[:, D_MODEL + KV_DIM:]
    k = k * _head_rms(k, e_ref[...], et_ref[...], KV_DIM) * kn_ref[...]
    if keep_f32:
        refs[3][0] = k
        refs[4][0] = v
    if rope:
        lane = lax.broadcasted_iota(jnp.int32, (x_ref.shape[1], V7X_LANES), 1)
        k = _rope(k, cos_ref[...], sin_ref[...], (lane % 32) < 16)
    k_ref[0] = k.astype(BF16)
    v_ref[0] = v.T.astype(BF16)
    qt = qkv[:, :D_MODEL].T
    quarter = HEAD_DIM // 4
    for hd in range(N_HEADS):
        qh = qt[hd * HEAD_DIM:(hd + 1) * HEAD_DIM]
        ms = jnp.sum(qh * qh, axis=0, keepdims=True) * (1.0 / HEAD_DIM)
        qh = qh * lax.rsqrt(ms + EPS) * qn_ref[...]
        if rope:
            parts = [qh[i * quarter:(i + 1) * quarter] for i in range(4)]
            partner = jnp.concatenate([parts[1], parts[0], parts[3], parts[2]], axis=0)
            qh = qh * cost_ref[...] + partner * sint_ref[...]
        q_ref[0, hd * HEAD_DIM:(hd + 1) * HEAD_DIM, :] = qh.astype(BF16)


def _head_selectors():
    col = np.arange(D_MODEL)[:, None] // HEAD_DIM
    e = (col == np.arange(V7X_LANES)[None, :]).astype(np.float32)
    return jnp.asarray(e, BF16), jnp.asarray(e.T, BF16)


def _rope_tables(L):
    half = HEAD_DIM // 2
    rows = jnp.repeat(jnp.arange(L // GRID_W), GRID_W)
    cols = jnp.tile(jnp.arange(GRID_W), L // GRID_W)
    inv_freq = 1.0 / (ROPE_THETA ** (jnp.arange(0, half, 2, dtype=F32) / half))
    ar = rows.astype(F32)[:, None] * inv_freq
    ac = cols.astype(F32)[:, None] * inv_freq
    cos = jnp.concatenate([jnp.cos(ar), jnp.cos(ar), jnp.cos(ac), jnp.cos(ac)], axis=1)
    sin = jnp.concatenate([-jnp.sin(ar), jnp.sin(ar), -jnp.sin(ac), jnp.sin(ac)], axis=1)
    return jnp.tile(cos, (1, 2)), jnp.tile(sin, (1, 2)), cos.T, sin.T


def _qkv_layer(x, g, sh, sc, w_qkv, q_norm, k_norm, tm, rope):
    B, L, D = x.shape
    keep_f32 = not rope
    e, et = _head_selectors()
    qn = jnp.broadcast_to((q_norm * (HEAD_DIM ** -0.5 * LOG2E))[:, None], (HEAD_DIM, tm))
    kn = jnp.tile(k_norm, N_KV_HEADS).reshape(1, KV_DIM)
    ms = _mod_spec(sh.shape[0])
    in_specs = [
        pl.BlockSpec((1, tm, D), lambda b, t: (b, t, 0)),
        pl.BlockSpec((1, D), _CONST2), ms, ms,
        pl.BlockSpec((D, QKV_DIM), _CONST2),
        pl.BlockSpec((HEAD_DIM, tm), _CONST2),
        pl.BlockSpec((1, KV_DIM), _CONST2),
        pl.BlockSpec((KV_DIM, V7X_LANES), _CONST2),
        pl.BlockSpec((V7X_LANES, KV_DIM), _CONST2),
    ]
    args = [x, g.reshape(1, D), sh, sc, w_qkv, qn, kn, e[:KV_DIM], et[:, :KV_DIM]]
    if rope:
        cos, sin, cos_t, sin_t = _rope_tables(L)
        in_specs += [pl.BlockSpec((tm, V7X_LANES), lambda b, t: (t, 0))] * 2
        in_specs += [pl.BlockSpec((HEAD_DIM, tm), lambda b, t: (0, t))] * 2
        args += [cos, sin, cos_t, sin_t]
    tok = lambda w: pl.BlockSpec((1, tm, w), lambda b, t: (b, t, 0))
    tok_t = lambda w: pl.BlockSpec((1, w, tm), lambda b, t: (b, 0, t))
    out_specs = [tok_t(D), tok(KV_DIM), tok_t(KV_DIM)]
    out_shape = [jax.ShapeDtypeStruct((B, D, L), BF16), jax.ShapeDtypeStruct((B, L, KV_DIM), BF16),
                 jax.ShapeDtypeStruct((B, KV_DIM, L), BF16)]
    if keep_f32:
        out_specs += [tok(KV_DIM), tok(KV_DIM)]
        out_shape += [jax.ShapeDtypeStruct((B, L, KV_DIM), F32)] * 2
    est = 2 * D * QKV_DIM * 2 + 2 * tm * D * 4 + 10 * tm * QKV_DIM * 4
    return pl.pallas_call(
        functools.partial(_qkv_kernel, rope=rope, keep_f32=keep_f32),
        grid=(B, L // tm),
        in_specs=in_specs,
        out_specs=out_specs,
        out_shape=out_shape,
        compiler_params=pltpu.CompilerParams(
            dimension_semantics=("arbitrary", "arbitrary"), vmem_limit_bytes=_vmem_limit(est)),
        name="qkv_rope" if rope else "qkv",
    )(*args)


ONES_ROWS = V7X_BF16_ROWS


def _attend(qt_ref, parts, sink_ref, o_ref, ot_ref):
    for kh in range(N_KV_HEADS):
        heads = [kh * Q_PER_KV + g for g in range(Q_PER_KV)]
        qt = jnp.concatenate([qt_ref[0, h * HEAD_DIM:(h + 1) * HEAD_DIM, :] for h in heads], axis=1)
        sink = jnp.concatenate([jnp.full((1, BLOCK), sink_ref[h] * LOG2E, F32) for h in heads], axis=1)
        lo, hi = kh * HEAD_DIM, (kh + 1) * HEAD_DIM
        scores = []
        m = sink
        for k, _, mask in parts:
            s = _dot(k[:, lo:hi], qt)
            if mask is not None:
                s = jnp.where(mask, s, NEG_INF)
            scores.append(s)
            m = jnp.maximum(m, jnp.max(s, axis=0, keepdims=True))
        acc = jnp.zeros((HEAD_DIM + ONES_ROWS, Q_PER_KV * BLOCK), F32)
        for s, (_, vt, _) in zip(scores, parts):
            p = jnp.exp2(s - m).astype(BF16)
            vt_ext = jnp.concatenate([vt[lo:hi], jnp.ones((ONES_ROWS, vt.shape[1]), BF16)], axis=0)
            acc = acc + _dot(vt_ext, p)
        denom = jnp.exp2(sink - m) + acc[HEAD_DIM:HEAD_DIM + 1]
        out = acc[:HEAD_DIM] / denom
        for g, h in enumerate(heads):
            ot_ref[h * HEAD_DIM:(h + 1) * HEAD_DIM, :] = out[:, g * BLOCK:(g + 1) * BLOCK]
    o_ref[0] = ot_ref[...].T.astype(BF16)


def _attn_latent_kernel(sink_ref, qt_ref, kp_ref, kc_ref, kn_ref, vp_ref, vc_ref, vn_ref, ck_ref, cv_ref,
                        o_ref, ot_ref, *, nqb):
    n = pl.program_id(1)
    kw = jnp.concatenate([kp_ref[0], kc_ref[0], kn_ref[0]], axis=0)
    vw = jnp.concatenate([vp_ref[0], vc_ref[0], vn_ref[0]], axis=1)
    shape = (3 * BLOCK, Q_PER_KV * BLOCK)
    sj = lax.broadcasted_iota(jnp.int32, shape, 0)
    qi = lax.broadcasted_iota(jnp.int32, shape, 1) % BLOCK
    band = jnp.abs(sj - BLOCK - qi) <= WINDOW
    lo = jnp.where(n > 0, 0, BLOCK)
    hi = jnp.where(n < nqb - 1, 3 * BLOCK, 2 * BLOCK)
    mask = band & (sj >= lo) & (sj < hi)
    _attend(qt_ref, [(kw, vw, mask), (ck_ref[0], cv_ref[0], None)], sink_ref, o_ref, ot_ref)


def _attn_context_kernel(sink_ref, qt_ref, k_ref, v_ref, o_ref, ot_ref):
    _attend(qt_ref, [(k_ref[0], v_ref[0], None)], sink_ref, o_ref, ot_ref)


def _attention(qt, k, vt, sink, ctx_k=None, ctx_vt=None):
    B, D, L = qt.shape
    nqb = L // BLOCK
    qspec = pl.BlockSpec((1, D, BLOCK), lambda b, n: (b, 0, n))
    smem = pl.BlockSpec(memory_space=pltpu.SMEM)
    if ctx_k is None:
        kspec = pl.BlockSpec((1, L, KV_DIM), lambda b, n: (b, 0, 0))
        vspec = pl.BlockSpec((1, KV_DIM, L), lambda b, n: (b, 0, 0))
        body, in_specs, args = _attn_context_kernel, [smem, qspec, kspec, vspec], (sink, qt, k, vt)
    else:
        P = ctx_k.shape[1]
        prev = lambda n: jnp.maximum(n - 1, 0)
        nxt = lambda n: jnp.minimum(n + 1, nqb - 1)
        kspecs = [pl.BlockSpec((1, BLOCK, KV_DIM), lambda b, n, f=f: (b, f(n), 0)) for f in (prev, lambda n: n, nxt)]
        vspecs = [pl.BlockSpec((1, KV_DIM, BLOCK), lambda b, n, f=f: (b, 0, f(n))) for f in (prev, lambda n: n, nxt)]
        body = functools.partial(_attn_latent_kernel, nqb=nqb)
        in_specs = [smem, qspec, *kspecs, *vspecs,
                    pl.BlockSpec((1, P, KV_DIM), lambda b, n: (b, 0, 0)),
                    pl.BlockSpec((1, KV_DIM, P), lambda b, n: (b, 0, 0))]
        args = (sink, qt, k, k, k, vt, vt, vt, ctx_k, ctx_vt)
    return pl.pallas_call(
        body,
        grid=(B, nqb),
        in_specs=in_specs,
        out_specs=pl.BlockSpec((1, BLOCK, D), lambda b, n: (b, n, 0)),
        out_shape=jax.ShapeDtypeStruct((B, L, D), BF16),
        scratch_shapes=[pltpu.VMEM((D, BLOCK), F32)],
        compiler_params=pltpu.CompilerParams(
            dimension_semantics=("arbitrary", "arbitrary"), vmem_limit_bytes=_vmem_limit(24 << 20)),
        name="attn_context" if ctx_k is None else "attn_latent",
    )(*args)


POOL_TM = 256
POOL_HALO = 8


def _pool_kernel(xm_ref, xp_ref, xn_ref, g_ref, sh_ref, sc_ref, gate_ref, band_ref, pw_ref, ps_ref, o_ref,
                 *, nt, seq_len):
    t = pl.program_id(1)
    gs = g_ref[...] * (1.0 + sc_ref[0])
    sh = sh_ref[0]
    xm = xm_ref[0]
    hm = _norm_mod(xm, gs, sh)
    hp = _norm_mod(xp_ref[0], gs, sh) * (t > 0).astype(F32)
    hn = _norm_mod(xn_ref[0], gs, sh) * (t < nt - 1).astype(F32)
    h_hi, h_lo = _split_bf16(jnp.concatenate([hp, hm, hn], axis=0))
    tpos = t * POOL_TM + lax.broadcasted_iota(jnp.int32, (POOL_TM, GROUP_DIM), 0)
    gate = gate_ref[0]
    for g, win in enumerate(POOL_WINDOWS):
        back = win // 2
        fwd = win - back - 1
        cs = slice(g * GROUP_DIM, (g + 1) * GROUP_DIM)
        band = band_ref[g]
        wsum = _dot(band, h_hi[:, cs]) + _dot(band, h_lo[:, cs])
        cnt = jnp.minimum(tpos + fwd + 1, seq_len) - jnp.maximum(tpos - back, 0)
        pooled = wsum / cnt.astype(F32) - hm[:, cs]
        mixed = _dot(pooled.astype(BF16), pw_ref[g])
        o_ref[0, :, cs] = xm[:, cs] + gate[:, cs] * (mixed * ps_ref[:, cs])


def _pool_bands():
    r = np.arange(POOL_TM)[:, None] + POOL_HALO
    c = np.arange(POOL_TM + 2 * POOL_HALO)[None, :]
    bands = [((c >= r - w // 2) & (c <= r + (w - w // 2 - 1))) for w in POOL_WINDOWS]
    return jnp.asarray(np.stack(bands).astype(np.float32), BF16)


def _pool_layer(x, g, sh, sc, gate, pool_w, pool_scale):
    B, L, D = x.shape
    tm = POOL_TM
    nt = L // tm
    hb = tm // POOL_HALO
    ms = _mod_spec(sh.shape[0])
    ext = tm + 2 * POOL_HALO
    return pl.pallas_call(
        functools.partial(_pool_kernel, nt=nt, seq_len=L),
        grid=(B, nt),
        in_specs=[
            pl.BlockSpec((1, tm, D), lambda b, t: (b, t, 0)),
            pl.BlockSpec((1, POOL_HALO, D), lambda b, t: (b, jnp.maximum(t * hb - 1, 0), 0)),
            pl.BlockSpec((1, POOL_HALO, D), lambda b, t: (b, jnp.minimum((t + 1) * hb, L // POOL_HALO - 1), 0)),
            pl.BlockSpec((1, D), _CONST2), ms, ms, ms,
            pl.BlockSpec((N_GROUPS, tm, ext), lambda b, t: (0, 0, 0)),
            pl.BlockSpec((N_GROUPS, GROUP_DIM, GROUP_DIM), lambda b, t: (0, 0, 0)),
            pl.BlockSpec((1, D), _CONST2),
        ],
        out_specs=pl.BlockSpec((1, tm, D), lambda b, t: (b, t, 0)),
        out_shape=jax.ShapeDtypeStruct((B, L, D), F32),
        compiler_params=pltpu.CompilerParams(
            dimension_semantics=("arbitrary", "arbitrary"), vmem_limit_bytes=_vmem_limit(16 << 20)),
        name="pool_mix",
    )(x, x, x, g.reshape(1, D), sh, sc, gate, _pool_bands(), pool_w, pool_scale.reshape(1, D))


def _dft_tables(n):
    k = np.arange(n)
    ang = 2.0 * np.pi * ((k[:, None] * k[None, :]) % n) / n
    return np.cos(ang), np.sin(ang)


def _bf16_table(a):
    return jnp.asarray(a, F32).astype(BF16)


def _chan_dft_kernel(x_ref, g_ref, sh_ref, sc_ref, cc_ref, sc_tab_ref, y_ref):
    gs = g_ref[...] * (1.0 + sc_ref[0])
    h = _norm_mod(x_ref[0], gs, sh_ref[0]).astype(BF16)
    for g in range(N_GROUPS):
        cs = slice(g * GROUP_DIM, (g + 1) * GROUP_DIM)
        y_ref[0, 0, :, cs] = _dot(h[:, cs], cc_ref[...]).astype(BF16)
        y_ref[0, 1, :, cs] = _dot(h[:, cs], sc_tab_ref[...]).astype(BF16)


def _chan_dft(x, g, sh, sc, tm):
    B, L, D = x.shape
    cc, sn = _dft_tables(GROUP_DIM)
    ms = _mod_spec(sh.shape[0])
    tab = pl.BlockSpec((GROUP_DIM, GROUP_DIM), _CONST2)
    return pl.pallas_call(
        _chan_dft_kernel,
        grid=(B, L // tm),
        in_specs=[pl.BlockSpec((1, tm, D), lambda b, t: (b, t, 0)), pl.BlockSpec((1, D), _CONST2), ms, ms, tab, tab],
        out_specs=pl.BlockSpec((1, 2, tm, D), lambda b, t: (b, 0, t, 0)),
        out_shape=jax.ShapeDtypeStruct((B, 2, L, D), BF16),
        compiler_params=pltpu.CompilerParams(
            dimension_semantics=("arbitrary", "arbitrary"), vmem_limit_bytes=_vmem_limit(12 * tm * D * 4)),
        name="chan_dft",
    )(x, g.reshape(1, D), sh, sc, _bf16_table(cc), _bf16_table(sn))


def _pos_dft_kernel(w_ref, y_ref, o_ref, *, scale):
    o_ref[0] = (_dot(w_ref[...], y_ref[0]) * scale).astype(BF16)


def _pos_dft(y, L, tmo, tn):
    B, _, D = y.shape
    cl, sl = _dft_tables(L)
    w = _bf16_table(np.concatenate([cl, -sl], axis=1))
    scale = 1.0 / math.sqrt(L * GROUP_DIM)
    est = 2 * tmo * 2 * L * 2 + 2 * 2 * L * tn * 2 + 3 * tmo * tn * 4
    return pl.pallas_call(
        functools.partial(_pos_dft_kernel, scale=scale),
        grid=(B, D // tn, L // tmo),
        in_specs=[
            pl.BlockSpec((tmo, 2 * L), lambda b, j, i: (i, 0)),
            pl.BlockSpec((1, 2 * L, tn), lambda b, j, i: (b, 0, j)),
        ],
        out_specs=pl.BlockSpec((1, tmo, tn), lambda b, j, i: (b, i, j)),
        out_shape=jax.ShapeDtypeStruct((B, L, D), BF16),
        compiler_params=pltpu.CompilerParams(
            dimension_semantics=("arbitrary", "arbitrary", "arbitrary"), vmem_limit_bytes=_vmem_limit(est)),
        name="pos_dft",
    )(w, y)


FFT_N = 64
FFT_T = 16


def _fft_stage1_kernel(x_ref, g_ref, sh_ref, sc_ref, cc_ref, ns_ref, tw_ref, y_ref, gbuf_ref):
    rows = FFT_N * FFT_T
    gs = g_ref[...] * (1.0 + sc_ref[0])
    h = _norm_mod(x_ref[0].reshape(rows, D_MODEL), gs, sh_ref[0]).astype(BF16)
    for g in range(N_GROUPS):
        cs = slice(g * GROUP_DIM, (g + 1) * GROUP_DIM)
        _put_cols(gbuf_ref.at[0], g * GROUP_DIM, slice(None), _dot(h[:, cs], cc_ref[...]))
        _put_cols(gbuf_ref.at[1], g * GROUP_DIM, slice(None), _dot(h[:, cs], ns_ref[...]))
    for t in range(FFT_T):
        sel = pl.ds(t, FFT_N, stride=FFT_T)
        gst = jnp.concatenate([_get_cols(gbuf_ref.at[0], sel), _get_cols(gbuf_ref.at[1], sel)], axis=0)
        y = _dot(tw_ref[t], gst.astype(BF16))
        _put_cols(gbuf_ref.at[0], 0, sel, y[:FFT_N])
        _put_cols(gbuf_ref.at[1], 0, sel, y[FFT_N:])
    for p in range(2):
        y_ref[0, p] = _get_cols(gbuf_ref.at[p], slice(None)).reshape(FFT_N, FFT_T, D_MODEL).astype(BF16)


def _get_cols(buf_ref, rows):
    return jnp.concatenate([buf_ref[c, rows, :] for c in range(buf_ref.shape[0])], axis=1)


def _put_cols(buf_ref, col0, rows, val):
    for c in range(val.shape[1] // V7X_LANES):
        buf_ref[col0 // V7X_LANES + c, rows, :] = val[:, c * V7X_LANES:(c + 1) * V7X_LANES]


def _fft_stage2_kernel(y_ref, x_ref, w_ref, gate_ref, r_ref, o_ref, zbuf_ref, *, scale):
    for f in range(FFT_T):
        yst = jnp.concatenate([y_ref[0, 0, f], y_ref[0, 1, f]], axis=0)
        _put_cols(zbuf_ref, 0, pl.ds(f, FFT_N, stride=FFT_T), _dot(r_ref[...], yst) * scale)
    mix = _dot(_get_cols(zbuf_ref, slice(None)).astype(BF16), w_ref[...])
    x = x_ref[0].reshape(FFT_N * FFT_T, D_MODEL)
    o_ref[0] = (x + gate_ref[0] * mix).reshape(FFT_N, FFT_T, D_MODEL)


def _fnet_fft_layer(x, g, sh, sc, gate, fnet_w):
    B, L, D = x.shape
    n, tt = FFT_N, FFT_T
    x4 = x.reshape(B, n, n, D)
    cc, sn = _dft_tables(GROUP_DIM)
    f2 = np.arange(n)[None, :, None]
    tpos = np.arange(n)[:, None, None] + n * np.arange(n)[None, None, :]
    ang = 2.0 * np.pi * ((f2 * tpos) % L) / L
    c1, s1 = np.cos(ang), np.sin(ang)
    tw = np.concatenate([np.concatenate([c1, s1], axis=2), np.concatenate([-s1, c1], axis=2)], axis=1)
    c2, s2 = _dft_tables(n)
    r = np.concatenate([c2, s2], axis=1)
    ms = _mod_spec(sh.shape[0])
    tab = pl.BlockSpec((GROUP_DIM, GROUP_DIM), _CONST2)
    y = pl.pallas_call(
        _fft_stage1_kernel,
        grid=(B, n // tt),
        in_specs=[
            pl.BlockSpec((1, n, tt, D), lambda b, i: (b, 0, i, 0)),
            pl.BlockSpec((1, D), _CONST2), ms, ms, tab, tab,
            pl.BlockSpec((tt, 2 * n, 2 * n), lambda b, i: (i, 0, 0)),
        ],
        out_specs=pl.BlockSpec((1, 2, n, tt, D), lambda b, i: (b, 0, 0, i, 0)),
        out_shape=jax.ShapeDtypeStruct((B, 2, n, n, D), BF16),
        scratch_shapes=[pltpu.VMEM((2, D // V7X_LANES, n * tt, V7X_LANES), F32)],
        compiler_params=pltpu.CompilerParams(
            dimension_semantics=("arbitrary", "arbitrary"),
            vmem_limit_bytes=_vmem_limit(2 * n * tt * D * (4 + 4 + 4) + 2 * n * tt * D * 4)),
        name="fft_stage1",
    )(x4, g.reshape(1, D), sh, sc, _bf16_table(cc), _bf16_table(-sn), _bf16_table(tw))
    out = pl.pallas_call(
        functools.partial(_fft_stage2_kernel, scale=1.0 / math.sqrt(L * GROUP_DIM)),
        grid=(B, n // tt),
        in_specs=[
            pl.BlockSpec((1, 2, tt, n, D), lambda b, j: (b, 0, j, 0, 0)),
            pl.BlockSpec((1, n, tt, D), lambda b, j: (b, 0, j, 0)),
            pl.BlockSpec((D, D), _CONST2),
            _mod_spec(gate.shape[0]),
            pl.BlockSpec((n, 2 * n), _CONST2),
        ],
        out_specs=pl.BlockSpec((1, n, tt, D), lambda b, j: (b, 0, j, 0)),
        out_shape=jax.ShapeDtypeStruct((B, n, n, D), F32),
        scratch_shapes=[pltpu.VMEM((D // V7X_LANES, n * tt, V7X_LANES), F32)],
        compiler_params=pltpu.CompilerParams(
            dimension_semantics=("arbitrary", "arbitrary"),
            vmem_limit_bytes=_vmem_limit(n * tt * D * (4 + 8 + 8 + 4 + 4) + 2 * D * D * 2)),
        name="fft_stage2",
    )(y, x4, fnet_w, gate, _bf16_table(r))
    return out.reshape(B, L, D)


def _trunk(x, mods, ctx_k, ctx_v, wts, tm):
    (norm_mix, norm_ffn, w_qkv, q_norm, k_norm, sink, w_o, pool_w, pool_scale, fnet_w,
     w_in, conv_w, conv_b, w_out) = wts
    B, L, D = x.shape
    latent = ctx_k is not None
    ks, vs = [], []
    for i in range(DEPTH):
        kind, j = i % N_MIXERS, i // N_MIXERS
        sh1, sc1, g1, sh2, sc2, g2 = mods[i]
        if kind == 0:
            outs = _qkv_layer(x, norm_mix[i], sh1, sc1, w_qkv[j], q_norm[j], k_norm[j], tm, rope=latent)
            q, k, v = outs[:3]
            if latent:
                o = _attention(q, k, v, sink[j], ctx_k[j], ctx_v[j])
            else:
                o = _attention(q, k, v, sink[j])
                ks.append(outs[3].reshape(B, L, N_KV_HEADS, HEAD_DIM))
                vs.append(outs[4].reshape(B, L, N_KV_HEADS, HEAD_DIM))
            x = _proj_res(x, o, w_o[j], g1, tm)
        elif kind == 1:
            x = _pool_layer(x, norm_mix[i], sh1, sc1, g1, pool_w[j], pool_scale[j])
        elif L == FFT_N * FFT_N:
            x = _fnet_fft_layer(x, norm_mix[i], sh1, sc1, g1, fnet_w[j])
        else:
            y = _chan_dft(x, norm_mix[i], sh1, sc1, tm).reshape(B, 2 * L, D)
            f = _pos_dft(y, L, min(L, 512), 512 if L > 512 else D)
            x = _proj_res(x, f, fnet_w[j], g1, tm)
        x = _ffn_layer(x, norm_ffn[i], sh2, sc2, g2, w_in[i], conv_w[i], conv_b[i], w_out[i], tm)
    return x, ks, vs


def kernel(x_prompt, x_sample, cache_k, cache_v, c, c_ctx, norm_mix, norm_ffn, ada_w, ada_b, attn_w_qkv,
           attn_q_norm, attn_k_norm, attn_sink, attn_w_o, pool_w, pool_scale, fnet_w, ffn_w_in, ffn_conv_w,
           ffn_conv_b, ffn_w_out):
    nb = c.shape[0]
    cond = jnp.concatenate([c, c_ctx[None], jnp.zeros((ADA_ROWS - nb - 1, D_MODEL), F32)], axis=0)
    mods = _ada_mods(cond, ada_w, ada_b).reshape(DEPTH, ADA_ROWS, 6, 1, D_MODEL)
    mods_sample = [[mods[i, :nb, m] for m in range(6)] for i in range(DEPTH)]
    mods_prompt = [[mods[i, nb:nb + 1, m] for m in range(6)] for i in range(DEPTH)]
    wts = (norm_mix, norm_ffn, attn_w_qkv.astype(BF16), attn_q_norm, attn_k_norm, attn_sink,
           attn_w_o.astype(BF16), pool_w.astype(BF16), pool_scale, fnet_w.astype(BF16),
           ffn_w_in.astype(BF16), ffn_conv_w, ffn_conv_b, ffn_w_out.astype(BF16))
    y_prompt, ks, vs = _trunk(x_prompt, mods_prompt, None, None, wts, tm=256)
    past = cache_k.shape[2]
    ctx_k = [cache_k[:, j].reshape(nb, past, KV_DIM).astype(BF16) for j in range(cache_k.shape[1])]
    ctx_v = [cache_v[:, j].reshape(nb, past, KV_DIM).astype(BF16).transpose(0, 2, 1) for j in range(cache_v.shape[1])]
    y_sample, _, _ = _trunk(x_sample, mods_sample, ctx_k, ctx_v, wts, tm=512)
    return (y_prompt, y_sample, jnp.stack(ks, axis=1), jnp.stack(vs, axis=1))
```

```python
import functools
import math

import jax
import jax.numpy as jnp
import numpy as np
from jax import lax
from jax.experimental import pallas as pl
from jax.experimental.pallas import tpu as pltpu

D_MODEL = 1024
DEPTH = 4
GRID_W = 64
N_MIXERS = 3
N_HEADS = 16
N_KV_HEADS = 4
HEAD_DIM = D_MODEL // N_HEADS
Q_PER_KV = N_HEADS // N_KV_HEADS
KV_DIM = N_KV_HEADS * HEAD_DIM
QKV_DIM = (N_HEADS + 2 * N_KV_HEADS) * HEAD_DIM
WINDOW = 128
BLOCK = 128
ROPE_THETA = 10000.0
POOL_WINDOWS = (2, 4, 8, 16)
N_GROUPS = 4
GROUP_DIM = D_MODEL // N_GROUPS
D_FF = ((8 * D_MODEL // 3 + 127) // 128) * 128
EPS = 1e-6
NEG_INF = -1e30
LOG2E = math.log2(math.e)

V7X_LANES = 128
V7X_SUBLANES = 8
V7X_BF16_ROWS = 16
V7X_MXU_DIM = 256
V7X_VMEM_BYTES = 64 * 1024 * 1024

F32 = jnp.float32
BF16 = jnp.bfloat16


def _vmem_limit(nbytes):
    return int(min(V7X_VMEM_BYTES - (4 << 20), nbytes * 3 // 2 + (4 << 20)))


def _silu(x):
    return x * (1.0 / (1.0 + jnp.exp(-x)))


def _split_bf16(a):
    hi = a.astype(BF16)
    lo = (a - hi.astype(F32)).astype(BF16)
    return hi, lo


def _norm_mod(x, gs, sh):
    ms = jnp.mean(x * x, axis=-1, keepdims=True)
    return x * lax.rsqrt(ms + EPS) * gs + sh


def _dot(a, b):
    return jnp.dot(a, b, preferred_element_type=F32)


ADA_ROWS = 16
ADA_TN = 1536


def _ada_kernel(c_ref, w_ref, b_ref, o_ref):
    s_hi, s_lo = _split_bf16(_silu(c_ref[...]))
    w_hi, w_lo = _split_bf16(w_ref[0])
    acc = _dot(s_hi, w_hi) + (_dot(s_lo, w_hi) + _dot(s_hi, w_lo))
    o_ref[0] = acc + b_ref[0]


def _ada_mods(cond, ada_w, ada_b):
    n_out = 6 * D_MODEL
    blk = D_MODEL * ADA_TN * 4
    return pl.pallas_call(
        _ada_kernel,
        grid=(DEPTH, n_out // ADA_TN),
        in_specs=[
            pl.BlockSpec((ADA_ROWS, D_MODEL), lambda i, j: (0, 0)),
            pl.BlockSpec((1, D_MODEL, ADA_TN), lambda i, j: (i, 0, j)),
            pl.BlockSpec((1, 1, ADA_TN), lambda i, j: (i, 0, j)),
        ],
        out_specs=pl.BlockSpec((1, ADA_ROWS, ADA_TN), lambda i, j: (i, 0, j)),
        out_shape=jax.ShapeDtypeStruct((DEPTH, ADA_ROWS, n_out), F32),
        compiler_params=pltpu.CompilerParams(
            dimension_semantics=("arbitrary", "arbitrary"),
            vmem_limit_bytes=_vmem_limit(2 * blk + 2 * blk),
        ),
        name="ada_mods",
    )(cond, ada_w, ada_b.reshape(DEPTH, 1, n_out))


FFN_CN = 256
FFN_HALO = 8
FFN_RC = 128


def _shift_rows(u, edge_row, down):
    n = u.shape[0]
    rid = lax.broadcasted_iota(jnp.int32, (V7X_SUBLANES, u.shape[1]), 0)
    if down:
        r = pltpu.roll(u, 1, 0)
        first = jnp.where(rid == 0, edge_row, r[:V7X_SUBLANES])
        return jnp.concatenate([first, r[V7X_SUBLANES:]], axis=0)
    r = pltpu.roll(u, n - 1, 0)
    last = jnp.where(rid == V7X_SUBLANES - 1, edge_row, r[n - V7X_SUBLANES:])
    return jnp.concatenate([r[:n - V7X_SUBLANES], last], axis=0)


def _ffn_kernel(xm_ref, xp_ref, xn_ref, g_ref, sh_ref, sc_ref, gate_ref, win_ref, cw_ref, cb_ref,
                wout_ref, o_ref, h_ref, a_ref, *, tm, nt):
    t = pl.program_id(1)
    gs = g_ref[...] * (1.0 + sc_ref[0])
    sh = sh_ref[0]

    for r in range(0, tm, FFN_RC):
        h_ref[r:r + FFN_RC, :] = _norm_mod(xm_ref[0, r:r + FFN_RC, :], gs, sh).astype(BF16)
    hp = _norm_mod(xp_ref[0], gs, sh) * (t > 0).astype(F32)
    hn = _norm_mod(xn_ref[0], gs, sh) * (t < nt - 1).astype(F32)
    h_ref[tm:tm + 2 * FFN_HALO, :] = jnp.concatenate([hp, hn], axis=0).astype(BF16)

    def conv(u, c0):
        um = u[:tm]
        prev_row = u[tm + FFN_HALO - 1:tm + FFN_HALO]
        next_row = u[tm + FFN_HALO:tm + FFN_HALO + 1]
        w = cw_ref[:, c0:c0 + FFN_CN]
        return (_shift_rows(um, prev_row, True) * w[0:1] + um * w[1:2]
                + _shift_rows(um, next_row, False) * w[2:3] + cb_ref[:, c0:c0 + FFN_CN])

    for j in range(D_FF // FFN_CN):
        c0 = j * FFN_CN
        hs = h_ref[...]
        gate = conv(_dot(hs, win_ref[:, c0:c0 + FFN_CN]), c0)
        val = conv(_dot(hs, win_ref[:, D_FF + c0:D_FF + c0 + FFN_CN]), D_FF + c0)
        a_ref[:, c0:c0 + FFN_CN] = (_silu(gate) * val).astype(BF16)

    o_ref[0] = xm_ref[0] + gate_ref[0] * _dot(a_ref[...], wout_ref[...])


def _ffn_layer(x, g, sh, sc, gate, w_in, conv_w, conv_b, w_out, tm):
    B, L, D = x.shape
    nt = L // tm
    bm = sh.shape[0]
    hb = tm // FFN_HALO
    mod_spec = pl.BlockSpec((1, 1, D), (lambda b, t: (b, 0, 0)) if bm > 1 else (lambda b, t: (0, 0, 0)))
    const = lambda b, t: (0, 0)
    est = (2 * D * 2 * D_FF * 2 + 2 * D_FF * D * 2 + 4 * tm * D * 4
           + (tm + 16) * D * 2 + tm * D_FF * 2 + 6 * (tm + 16) * FFN_CN * 4)
    return pl.pallas_call(
        functools.partial(_ffn_kernel, tm=tm, nt=nt),
        grid=(B, nt),
        in_specs=[
            pl.BlockSpec((1, tm, D), lambda b, t: (b, t, 0)),
            pl.BlockSpec((1, FFN_HALO, D), lambda b, t: (b, jnp.maximum(t * hb - 1, 0), 0)),
            pl.BlockSpec((1, FFN_HALO, D), lambda b, t: (b, jnp.minimum((t + 1) * hb, L // FFN_HALO - 1), 0)),
            pl.BlockSpec((1, D), const),
            mod_spec, mod_spec, mod_spec,
            pl.BlockSpec((D, 2 * D_FF), const),
            pl.BlockSpec((3, 2 * D_FF), const),
            pl.BlockSpec((1, 2 * D_FF), const),
            pl.BlockSpec((D_FF, D), const),
        ],
        out_specs=pl.BlockSpec((1, tm, D), lambda b, t: (b, t, 0)),
        out_shape=jax.ShapeDtypeStruct((B, L, D), F32),
        scratch_shapes=[
            pltpu.VMEM((tm + 2 * FFN_HALO, D), BF16),
            pltpu.VMEM((tm, D_FF), BF16),
        ],
        compiler_params=pltpu.CompilerParams(
            dimension_semantics=("arbitrary", "arbitrary"),
            vmem_limit_bytes=_vmem_limit(est),
        ),
        name="conv_ffn",
    )(x, x, x, g.reshape(1, D), sh, sc, gate, w_in, conv_w, conv_b.reshape(1, 2 * D_FF), w_out)


def _mod_spec(bm):
    return pl.BlockSpec((1, 1, D_MODEL), (lambda b, t: (b, 0, 0)) if bm > 1 else (lambda b, t: (0, 0, 0)))


_CONST2 = lambda b, t: (0, 0)


def _proj_res_kernel(x_ref, a_ref, w_ref, gate_ref, o_ref):
    o_ref[0] = x_ref[0] + gate_ref[0] * _dot(a_ref[0], w_ref[...])


def _proj_res(x, a, w, gate, tm):
    B, L, D = x.shape
    kdim = a.shape[-1]
    est = 2 * kdim * D * 2 + 4 * tm * D * 4 + 2 * tm * kdim * 2 + tm * D * 4
    return pl.pallas_call(
        _proj_res_kernel,
        grid=(B, L // tm),
        in_specs=[
            pl.BlockSpec((1, tm, D), lambda b, t: (b, t, 0)),
            pl.BlockSpec((1, tm, kdim), lambda b, t: (b, t, 0)),
            pl.BlockSpec((kdim, D), _CONST2),
            _mod_spec(gate.shape[0]),
        ],
        out_specs=pl.BlockSpec((1, tm, D), lambda b, t: (b, t, 0)),
        out_shape=jax.ShapeDtypeStruct((B, L, D), F32),
        compiler_params=pltpu.CompilerParams(
            dimension_semantics=("arbitrary", "arbitrary"), vmem_limit_bytes=_vmem_limit(est)),
        name="proj_res",
    )(x, a, w, gate)


def _head_rms(y, e, et, width):
    s_hi, s_lo = _split_bf16(y * y)
    ss = _dot(s_hi, e[:width]) + _dot(s_lo, e[:width])
    r_hi, r_lo = _split_bf16(lax.rsqrt(ss * (1.0 / HEAD_DIM) + EPS))
    return _dot(r_hi, et[:, :width]) + _dot(r_lo, et[:, :width])


def _rope(y, cos, sin_signed, first):
    outs = []
    for c0 in range(0, y.shape[1], V7X_LANES):
        ys = y[:, c0:c0 + V7X_LANES]
        partner = jnp.where(first, pltpu.roll(ys, V7X_LANES - 16, 1), pltpu.roll(ys, 16, 1))
        outs.append(ys * cos + partner * sin_signed)
    return jnp.concatenate(outs, axis=1)


def _qkv_kernel(*refs, rope, keep_f32):
    x_ref, g_ref, sh_ref, sc_ref, w_ref, qn_ref, kn_ref, e_ref, et_ref = refs[:9]
    refs = refs[9:]
    if rope:
        cos_ref, sin_ref, cost_ref, sint_ref = refs[:4]
        refs = refs[4:]
    q_ref, k_ref, v_ref = refs[:3]
    gs = g_ref[...] * (1.0 + sc_ref[0])
    h = _norm_mod(x_ref[0], gs, sh_ref[0]).astype(BF16)
    qkv = _dot(h, w_ref[...])
    k = qkv[:, D_MODEL:D_MODEL + KV_DIM]
    v = qkv[:, D_MODEL + KV_DIM:]
    k = k * _head_rms(k, e_ref[...], et_ref[...], KV_DIM) * kn_ref[...]
    if keep_f32:
        refs[3][0] = k
        refs[4][0] = v
    if rope:
        lane = lax.broadcasted_iota(jnp.int32, (x_ref.shape[1], V7X_LANES), 1)
        k = _rope(k, cos_ref[...], sin_ref[...], (lane % 32) < 16)
    k_ref[0] = k.astype(BF16)
    v_ref[0] = v.T.astype(BF16)
    qt = qkv[:, :D_MODEL].T
    quarter = HEAD_DIM // 4
    for hd in range(N_HEADS):
        qh = qt[hd * HEAD_DIM:(hd + 1) * HEAD_DIM]
        ms = jnp.sum(qh * qh, axis=0, keepdims=True) * (1.0 / HEAD_DIM)
        qh = qh * lax.rsqrt(ms + EPS) * qn_ref[...]
        if rope:
            parts = [qh[i * quarter:(i + 1) * quarter] for i in range(4)]
            partner = jnp.concatenate([parts[1], parts[0], parts[3], parts[2]], axis=0)
            qh = qh * cost_ref[...] + partner * sint_ref[...]
        q_ref[0, hd * HEAD_DIM:(hd + 1) * HEAD_DIM, :] = qh.astype(BF16)


def _head_selectors():
    col = np.arange(D_MODEL)[:, None] // HEAD_DIM
    e = (col == np.arange(V7X_LANES)[None, :]).astype(np.float32)
    return jnp.asarray(e, BF16), jnp.asarray(e.T, BF16)


def _rope_tables(L):
    half = HEAD_DIM // 2
    rows = jnp.repeat(jnp.arange(L // GRID_W), GRID_W)
    cols = jnp.tile(jnp.arange(GRID_W), L // GRID_W)
    inv_freq = 1.0 / (ROPE_THETA ** (jnp.arange(0, half, 2, dtype=F32) / half))
    ar = rows.astype(F32)[:, None] * inv_freq
    ac = cols.astype(F32)[:, None] * inv_freq
    cos = jnp.concatenate([jnp.cos(ar), jnp.cos(ar), jnp.cos(ac), jnp.cos(ac)], axis=1)
    sin = jnp.concatenate([-jnp.sin(ar), jnp.sin(ar), -jnp.sin(ac), jnp.sin(ac)], axis=1)
    return jnp.tile(cos, (1, 2)), jnp.tile(sin, (1, 2)), cos.T, sin.T


def _qkv_layer(x, g, sh, sc, w_qkv, q_norm, k_norm, tm, rope):
    B, L, D = x.shape
    keep_f32 = not rope
    e, et = _head_selectors()
    qn = jnp.broadcast_to((q_norm * (HEAD_DIM ** -0.5 * LOG2E))[:, None], (HEAD_DIM, tm))
    kn = jnp.tile(k_norm, N_KV_HEADS).reshape(1, KV_DIM)
    ms = _mod_spec(sh.shape[0])
    in_specs = [
        pl.BlockSpec((1, tm, D), lambda b, t: (b, t, 0)),
        pl.BlockSpec((1, D), _CONST2), ms, ms,
        pl.BlockSpec((D, QKV_DIM), _CONST2),
        pl.BlockSpec((HEAD_DIM, tm), _CONST2),
        pl.BlockSpec((1, KV_DIM), _CONST2),
        pl.BlockSpec((KV_DIM, V7X_LANES), _CONST2),
        pl.BlockSpec((V7X_LANES, KV_DIM), _CONST2),
    ]
    args = [x, g.reshape(1, D), sh, sc, w_qkv, qn, kn, e[:KV_DIM], et[:, :KV_DIM]]
    if rope:
        cos, sin, cos_t, sin_t = _rope_tables(L)
        in_specs += [pl.BlockSpec((tm, V7X_LANES), lambda b, t: (t, 0))] * 2
        in_specs += [pl.BlockSpec((HEAD_DIM, tm), lambda b, t: (0, t))] * 2
        args += [cos, sin, cos_t, sin_t]
    tok = lambda w: pl.BlockSpec((1, tm, w), lambda b, t: (b, t, 0))
    tok_t = lambda w: pl.BlockSpec((1, w, tm), lambda b, t: (b, 0, t))
    out_specs = [tok_t(D), tok(KV_DIM), tok_t(KV_DIM)]
    out_shape = [jax.ShapeDtypeStruct((B, D, L), BF16), jax.ShapeDtypeStruct((B, L, KV_DIM), BF16),
                 jax.ShapeDtypeStruct((B, KV_DIM, L), BF16)]
    if keep_f32:
        out_specs += [tok(KV_DIM), tok(KV_DIM)]
        out_shape += [jax.ShapeDtypeStruct((B, L, KV_DIM), F32)] * 2
    est = 2 * D * QKV_DIM * 2 + 2 * tm * D * 4 + 10 * tm * QKV_DIM * 4
    return pl.pallas_call(
        functools.partial(_qkv_kernel, rope=rope, keep_f32=keep_f32),
        grid=(B, L // tm),
        in_specs=in_specs,
        out_specs=out_specs,
        out_shape=out_shape,
        compiler_params=pltpu.CompilerParams(
            dimension_semantics=("arbitrary", "arbitrary"), vmem_limit_bytes=_vmem_limit(est)),
        name="qkv_rope" if rope else "qkv",
    )(*args)


ONES_ROWS = V7X_BF16_ROWS


def _attend(qt_ref, parts, sink_ref, x_ref, gate_ref, wo_ref, o_ref, ot_ref):
    staged = []
    for kh in range(N_KV_HEADS):
        heads = [kh * Q_PER_KV + g for g in range(Q_PER_KV)]
        qt = jnp.concatenate([qt_ref[0, h * HEAD_DIM:(h + 1) * HEAD_DIM, :] for h in heads], axis=1)
        sink = jnp.concatenate([jnp.full((1, BLOCK), sink_ref[h] * LOG2E, F32) for h in heads], axis=1)
        scores = []
        m = sink
        for k, _, bias in parts:
            s = _dot(k[:, kh * HEAD_DIM:(kh + 1) * HEAD_DIM], qt)
            if bias is not None:
                s = s + bias
            scores.append(s)
            m = jnp.maximum(m, jnp.max(s, axis=0, keepdims=True))
        staged.append((heads, sink, m, scores))
    for kh, (heads, sink, m, scores) in enumerate(staged):
        acc = jnp.zeros((HEAD_DIM + ONES_ROWS, Q_PER_KV * BLOCK), F32)
        for s, (_, vt, _) in zip(scores, parts):
            p = jnp.exp2(s - m).astype(BF16)
            vt_ext = jnp.concatenate([vt[kh * HEAD_DIM:(kh + 1) * HEAD_DIM],
                                      jnp.ones((ONES_ROWS, vt.shape[1]), BF16)], axis=0)
            acc = acc + _dot(vt_ext, p)
        denom = jnp.exp2(sink - m) + acc[HEAD_DIM:HEAD_DIM + 1]
        out = acc[:HEAD_DIM] / denom
        for g, h in enumerate(heads):
            ot_ref[h * HEAD_DIM:(h + 1) * HEAD_DIM, :] = out[:, g * BLOCK:(g + 1) * BLOCK]
    attn = ot_ref[...].T.astype(BF16)
    o_ref[0] = x_ref[0] + gate_ref[0] * _dot(attn, wo_ref[...])


def _attn_latent_kernel(sink_ref, qt_ref, kp_ref, kc_ref, kn_ref, vp_ref, vc_ref, vn_ref, ck_ref, cv_ref,
                        bias_ref, x_ref, gate_ref, wo_ref, o_ref, ot_ref):
    kw = jnp.concatenate([kp_ref[0], kc_ref[0], kn_ref[0]], axis=0)
    vw = jnp.concatenate([vp_ref[0], vc_ref[0], vn_ref[0]], axis=1)
    _attend(qt_ref, [(kw, vw, bias_ref[0]), (ck_ref[0], cv_ref[0], None)], sink_ref,
            x_ref, gate_ref, wo_ref, o_ref, ot_ref)


def _attn_context_kernel(sink_ref, qt_ref, k_ref, v_ref, x_ref, gate_ref, wo_ref, o_ref, ot_ref):
    _attend(qt_ref, [(k_ref[0], v_ref[0], None)], sink_ref, x_ref, gate_ref, wo_ref, o_ref, ot_ref)


def _band_bias():
    sj = np.arange(3 * BLOCK)[:, None]
    qi = np.arange(Q_PER_KV * BLOCK)[None, :] % BLOCK
    band = np.abs(sj - BLOCK - qi) <= WINDOW
    first = band & (sj >= BLOCK)
    last = band & (sj < 2 * BLOCK)
    return jnp.asarray(np.where(np.stack([first, band, last]), 0.0, NEG_INF), F32)


def _attention(x, gate, w_o, qt, k, vt, sink, ctx_k=None, ctx_vt=None):
    B, D, L = qt.shape
    nqb = L // BLOCK
    qspec = pl.BlockSpec((1, D, BLOCK), lambda b, n: (b, 0, n))
    xspec = pl.BlockSpec((1, BLOCK, D), lambda b, n: (b, n, 0))
    smem = pl.BlockSpec(memory_space=pltpu.SMEM)
    tail_specs = [xspec, _mod_spec(gate.shape[0]), pl.BlockSpec((D, D), _CONST2)]
    if ctx_k is None:
        kspec = pl.BlockSpec((1, L, KV_DIM), lambda b, n: (b, 0, 0))
        vspec = pl.BlockSpec((1, KV_DIM, L), lambda b, n: (b, 0, 0))
        body, in_specs, args = _attn_context_kernel, [smem, qspec, kspec, vspec], (sink, qt, k, vt)
    else:
        P = ctx_k.shape[1]
        prev = lambda n: jnp.maximum(n - 1, 0)
        nxt = lambda n: jnp.minimum(n + 1, nqb - 1)
        which = lambda n: jnp.where(n == 0, 0, jnp.where(n == nqb - 1, 2, 1))
        kspecs = [pl.BlockSpec((1, BLOCK, KV_DIM), lambda b, n, f=f: (b, f(n), 0)) for f in (prev, lambda n: n, nxt)]
        vspecs = [pl.BlockSpec((1, KV_DIM, BLOCK), lambda b, n, f=f: (b, 0, f(n))) for f in (prev, lambda n: n, nxt)]
        body = _attn_latent_kernel
        in_specs = [smem, qspec, *kspecs, *vspecs,
                    pl.BlockSpec((1, P, KV_DIM), lambda b, n: (b, 0, 0)),
                    pl.BlockSpec((1, KV_DIM, P), lambda b, n: (b, 0, 0)),
                    pl.BlockSpec((1, 3 * BLOCK, Q_PER_KV * BLOCK), lambda b, n: (which(n), 0, 0))]
        args = (sink, qt, k, k, k, vt, vt, vt, ctx_k, ctx_vt, _band_bias())
    return pl.pallas_call(
        body,
        grid=(B, nqb),
        in_specs=in_specs + tail_specs,
        out_specs=xspec,
        out_shape=jax.ShapeDtypeStruct((B, L, D), F32),
        scratch_shapes=[pltpu.VMEM((D, BLOCK), F32)],
        compiler_params=pltpu.CompilerParams(
            dimension_semantics=("arbitrary", "arbitrary"), vmem_limit_bytes=_vmem_limit(28 << 20)),
        name="attn_context" if ctx_k is None else "attn_latent",
    )(*args, x, gate, w_o)


POOL_TM = 256
POOL_HALO = 8


def _pool_kernel(xm_ref, xp_ref, xn_ref, g_ref, sh_ref, sc_ref, gate_ref, band_ref, pw_ref, ps_ref, o_ref,
                 *, nt, seq_len):
    t = pl.program_id(1)
    gs = g_ref[...] * (1.0 + sc_ref[0])
    sh = sh_ref[0]
    xm = xm_ref[0]
    hm = _norm_mod(xm, gs, sh)
    hp = _norm_mod(xp_ref[0], gs, sh) * (t > 0).astype(F32)
    hn = _norm_mod(xn_ref[0], gs, sh) * (t < nt - 1).astype(F32)
    h_hi, h_lo = _split_bf16(jnp.concatenate([hp, hm, hn], axis=0))
    tpos = t * POOL_TM + lax.broadcasted_iota(jnp.int32, (POOL_TM, GROUP_DIM), 0)
    gate = gate_ref[0]
    for g, win in enumerate(POOL_WINDOWS):
        back = win // 2
        fwd = win - back - 1
        cs = slice(g * GROUP_DIM, (g + 1) * GROUP_DIM)
        band = band_ref[g]
        wsum = _dot(band, h_hi[:, cs]) + _dot(band, h_lo[:, cs])
        cnt = jnp.minimum(tpos + fwd + 1, seq_len) - jnp.maximum(tpos - back, 0)
        pooled = wsum / cnt.astype(F32) - hm[:, cs]
        mixed = _dot(pooled.astype(BF16), pw_ref[g])
        o_ref[0, :, cs] = xm[:, cs] + gate[:, cs] * (mixed * ps_ref[:, cs])


def _pool_bands():
    r = np.arange(POOL_TM)[:, None] + POOL_HALO
    c = np.arange(POOL_TM + 2 * POOL_HALO)[None, :]
    bands = [((c >= r - w // 2) & (c <= r + (w - w // 2 - 1))) for w in POOL_WINDOWS]
    return jnp.asarray(np.stack(bands).astype(np.float32), BF16)


def _pool_layer(x, g, sh, sc, gate, pool_w, pool_scale):
    B, L, D = x.shape
    tm = POOL_TM
    nt = L // tm
    hb = tm // POOL_HALO
    ms = _mod_spec(sh.shape[0])
    ext = tm + 2 * POOL_HALO
    return pl.pallas_call(
        functools.partial(_pool_kernel, nt=nt, seq_len=L),
        grid=(B, nt),
        in_specs=[
            pl.BlockSpec((1, tm, D), lambda b, t: (b, t, 0)),
            pl.BlockSpec((1, POOL_HALO, D), lambda b, t: (b, jnp.maximum(t * hb - 1, 0), 0)),
            pl.BlockSpec((1, POOL_HALO, D), lambda b, t: (b, jnp.minimum((t + 1) * hb, L // POOL_HALO - 1), 0)),
            pl.BlockSpec((1, D), _CONST2), ms, ms, ms,
            pl.BlockSpec((N_GROUPS, tm, ext), lambda b, t: (0, 0, 0)),
            pl.BlockSpec((N_GROUPS, GROUP_DIM, GROUP_DIM), lambda b, t: (0, 0, 0)),
            pl.BlockSpec((1, D), _CONST2),
        ],
        out_specs=pl.BlockSpec((1, tm, D), lambda b, t: (b, t, 0)),
        out_shape=jax.ShapeDtypeStruct((B, L, D), F32),
        compiler_params=pltpu.CompilerParams(
            dimension_semantics=("arbitrary", "arbitrary"), vmem_limit_bytes=_vmem_limit(16 << 20)),
        name="pool_mix",
    )(x, x, x, g.reshape(1, D), sh, sc, gate, _pool_bands(), pool_w, pool_scale.reshape(1, D))


def _dft_tables(n):
    k = np.arange(n)
    ang = 2.0 * np.pi * ((k[:, None] * k[None, :]) % n) / n
    return np.cos(ang), np.sin(ang)


def _bf16_table(a):
    return jnp.asarray(a, F32).astype(BF16)


def _chan_dft_kernel(x_ref, g_ref, sh_ref, sc_ref, cc_ref, sc_tab_ref, y_ref):
    gs = g_ref[...] * (1.0 + sc_ref[0])
    h = _norm_mod(x_ref[0], gs, sh_ref[0]).astype(BF16)
    for g in range(N_GROUPS):
        cs = slice(g * GROUP_DIM, (g + 1) * GROUP_DIM)
        y_ref[0, 0, :, cs] = _dot(h[:, cs], cc_ref[...]).astype(BF16)
        y_ref[0, 1, :, cs] = _dot(h[:, cs], sc_tab_ref[...]).astype(BF16)


def _chan_dft(x, g, sh, sc, tm):
    B, L, D = x.shape
    cc, sn = _dft_tables(GROUP_DIM)
    ms = _mod_spec(sh.shape[0])
    tab = pl.BlockSpec((GROUP_DIM, GROUP_DIM), _CONST2)
    return pl.pallas_call(
        _chan_dft_kernel,
        grid=(B, L // tm),
        in_specs=[pl.BlockSpec((1, tm, D), lambda b, t: (b, t, 0)), pl.BlockSpec((1, D), _CONST2), ms, ms, tab, tab],
        out_specs=pl.BlockSpec((1, 2, tm, D), lambda b, t: (b, 0, t, 0)),
        out_shape=jax.ShapeDtypeStruct((B, 2, L, D), BF16),
        compiler_params=pltpu.CompilerParams(
            dimension_semantics=("arbitrary", "arbitrary"), vmem_limit_bytes=_vmem_limit(12 * tm * D * 4)),
        name="chan_dft",
    )(x, g.reshape(1, D), sh, sc, _bf16_table(cc), _bf16_table(sn))


def _pos_dft_kernel(w_ref, y_ref, o_ref, *, scale):
    o_ref[0] = (_dot(w_ref[...], y_ref[0]) * scale).astype(BF16)


def _pos_dft(y, L, tmo, tn):
    B, _, D = y.shape
    cl, sl = _dft_tables(L)
    w = _bf16_table(np.concatenate([cl, -sl], axis=1))
    scale = 1.0 / math.sqrt(L * GROUP_DIM)
    est = 2 * tmo * 2 * L * 2 + 2 * 2 * L * tn * 2 + 3 * tmo * tn * 4
    return pl.pallas_call(
        functools.partial(_pos_dft_kernel, scale=scale),
        grid=(B, D // tn, L // tmo),
        in_specs=[
            pl.BlockSpec((tmo, 2 * L), lambda b, j, i: (i, 0)),
            pl.BlockSpec((1, 2 * L, tn), lambda b, j, i: (b, 0, j)),
        ],
        out_specs=pl.BlockSpec((1, tmo, tn), lambda b, j, i: (b, i, j)),
        out_shape=jax.ShapeDtypeStruct((B, L, D), BF16),
        compiler_params=pltpu.CompilerParams(
            dimension_semantics=("arbitrary", "arbitrary", "arbitrary"), vmem_limit_bytes=_vmem_limit(est)),
        name="pos_dft",
    )(w, y)


FFT_N = 64
FFT_T = 16


def _fft_stage1_kernel(x_ref, g_ref, sh_ref, sc_ref, cc_ref, ns_ref, tw_ref, y_ref, gbuf_ref):
    rows = FFT_N * FFT_T
    gs = g_ref[...] * (1.0 + sc_ref[0])
    h = _norm_mod(x_ref[0].reshape(rows, D_MODEL), gs, sh_ref[0]).astype(BF16)
    for g in range(N_GROUPS):
        cs = slice(g * GROUP_DIM, (g + 1) * GROUP_DIM)
        _put_cols(gbuf_ref.at[0], g * GROUP_DIM, slice(None), _dot(h[:, cs], cc_ref[...]))
        _put_cols(gbuf_ref.at[1], g * GROUP_DIM, slice(None), _dot(h[:, cs], ns_ref[...]))
    for t in range(FFT_T):
        sel = pl.ds(t, FFT_N, stride=FFT_T)
        gst = jnp.concatenate([_get_cols(gbuf_ref.at[0], sel), _get_cols(gbuf_ref.at[1], sel)], axis=0)
        y = _dot(tw_ref[t], gst.astype(BF16))
        _put_cols(gbuf_ref.at[0], 0, sel, y[:FFT_N])
        _put_cols(gbuf_ref.at[1], 0, sel, y[FFT_N:])
    for p in range(2):
        y_ref[0, p] = _get_cols(gbuf_ref.at[p], slice(None)).reshape(FFT_N, FFT_T, D_MODEL).astype(BF16)


def _get_cols(buf_ref, rows):
    return jnp.concatenate([buf_ref[c, rows, :] for c in range(buf_ref.shape[0])], axis=1)


def _put_cols(buf_ref, col0, rows, val):
    for c in range(val.shape[1] // V7X_LANES):
        buf_ref[col0 // V7X_LANES + c, rows, :] = val[:, c * V7X_LANES:(c + 1) * V7X_LANES]


def _fft_stage2_kernel(y_ref, x_ref, w_ref, gate_ref, r_ref, o_ref, zbuf_ref, *, scale):
    for f in range(FFT_T):
        yst = jnp.concatenate([y_ref[0, 0, f], y_ref[0, 1, f]], axis=0)
        _put_cols(zbuf_ref, 0, pl.ds(f, FFT_N, stride=FFT_T), _dot(r_ref[...], yst) * scale)
    mix = _dot(_get_cols(zbuf_ref, slice(None)).astype(BF16), w_ref[...])
    x = x_ref[0].reshape(FFT_N * FFT_T, D_MODEL)
    o_ref[0] = (x + gate_ref[0] * mix).reshape(FFT_N, FFT_T, D_MODEL)


def _fnet_fft_layer(x, g, sh, sc, gate, fnet_w):
    B, L, D = x.shape
    n, tt = FFT_N, FFT_T
    x4 = x.reshape(B, n, n, D)
    cc, sn = _dft_tables(GROUP_DIM)
    f2 = np.arange(n)[None, :, None]
    tpos = np.arange(n)[:, None, None] + n * np.arange(n)[None, None, :]
    ang = 2.0 * np.pi * ((f2 * tpos) % L) / L
    c1, s1 = np.cos(ang), np.sin(ang)
    tw = np.concatenate([np.concatenate([c1, s1], axis=2), np.concatenate([-s1, c1], axis=2)], axis=1)
    c2, s2 = _dft_tables(n)
    r = np.concatenate([c2, s2], axis=1)
    ms = _mod_spec(sh.shape[0])
    tab = pl.BlockSpec((GROUP_DIM, GROUP_DIM), _CONST2)
    y = pl.pallas_call(
        _fft_stage1_kernel,
        grid=(B, n // tt),
        in_specs=[
            pl.BlockSpec((1, n, tt, D), lambda b, i: (b, 0, i, 0)),
            pl.BlockSpec((1, D), _CONST2), ms, ms, tab, tab,
            pl.BlockSpec((tt, 2 * n, 2 * n), lambda b, i: (i, 0, 0)),
        ],
        out_specs=pl.BlockSpec((1, 2, n, tt, D), lambda b, i: (b, 0, 0, i, 0)),
        out_shape=jax.ShapeDtypeStruct((B, 2, n, n, D), BF16),
        scratch_shapes=[pltpu.VMEM((2, D // V7X_LANES, n * tt, V7X_LANES), F32)],
        compiler_params=pltpu.CompilerParams(
            dimension_semantics=("arbitrary", "arbitrary"),
            vmem_limit_bytes=_vmem_limit(2 * n * tt * D * (4 + 4 + 4) + 2 * n * tt * D * 4)),
        name="fft_stage1",
    )(x4, g.reshape(1, D), sh, sc, _bf16_table(cc), _bf16_table(-sn), _bf16_table(tw))
    out = pl.pallas_call(
        functools.partial(_fft_stage2_kernel, scale=1.0 / math.sqrt(L * GROUP_DIM)),
        grid=(B, n // tt),
        in_specs=[
            pl.BlockSpec((1, 2, tt, n, D), lambda b, j: (b, 0, j, 0, 0)),
            pl.BlockSpec((1, n, tt, D), lambda b, j: (b, 0, j, 0)),
            pl.BlockSpec((D, D), _CONST2),
            _mod_spec(gate.shape[0]),
            pl.BlockSpec((n, 2 * n), _CONST2),
        ],
        out_specs=pl.BlockSpec((1, n, tt, D), lambda b, j: (b, 0, j, 0)),
        out_shape=jax.ShapeDtypeStruct((B, n, n, D), F32),
        scratch_shapes=[pltpu.VMEM((D // V7X_LANES, n * tt, V7X_LANES), F32)],
        compiler_params=pltpu.CompilerParams(
            dimension_semantics=("arbitrary", "arbitrary"),
            vmem_limit_bytes=_vmem_limit(n * tt * D * (4 + 8 + 8 + 4 + 4) + 2 * D * D * 2)),
        name="fft_stage2",
    )(y, x4, fnet_w, gate, _bf16_table(r))
    return out.reshape(B, L, D)


def _trunk(x, mods, ctx_k, ctx_v, wts, tm):
    (norm_mix, norm_ffn, w_qkv, q_norm, k_norm, sink, w_o, pool_w, pool_scale, fnet_w,
     w_in, conv_w, conv_b, w_out) = wts
    B, L, D = x.shape
    latent = ctx_k is not None
    ks, vs = [], []
    for i in range(DEPTH):
        kind, j = i % N_MIXERS, i // N_MIXERS
        sh1, sc1, g1, sh2, sc2, g2 = mods[i]
        if kind == 0:
            outs = _qkv_layer(x, norm_mix[i], sh1, sc1, w_qkv[j], q_norm[j], k_norm[j], tm, rope=latent)
            q, k, v = outs[:3]
            if latent:
                x = _attention(x, g1, w_o[j], q, k, v, sink[j], ctx_k[j], ctx_v[j])
            else:
                x = _attention(x, g1, w_o[j], q, k, v, sink[j])
                ks.append(outs[3].reshape(B, L, N_KV_HEADS, HEAD_DIM))
                vs.append(outs[4].reshape(B, L, N_KV_HEADS, HEAD_DIM))
        elif kind == 1:
            x = _pool_layer(x, norm_mix[i], sh1, sc1, g1, pool_w[j], pool_scale[j])
        elif L == FFT_N * FFT_N:
            x = _fnet_fft_layer(x, norm_mix[i], sh1, sc1, g1, fnet_w[j])
        else:
            y = _chan_dft(x, norm_mix[i], sh1, sc1, tm).reshape(B, 2 * L, D)
            f = _pos_dft(y, L, min(L, 512), 512 if L > 512 else D)
            x = _proj_res(x, f, fnet_w[j], g1, tm)
        x = _ffn_layer(x, norm_ffn[i], sh2, sc2, g2, w_in[i], conv_w[i], conv_b[i], w_out[i], tm)
    return x, ks, vs


def kernel(x_prompt, x_sample, cache_k, cache_v, c, c_ctx, norm_mix, norm_ffn, ada_w, ada_b, attn_w_qkv,
           attn_q_norm, attn_k_norm, attn_sink, attn_w_o, pool_w, pool_scale, fnet_w, ffn_w_in, ffn_conv_w,
           ffn_conv_b, ffn_w_out):
    nb = c.shape[0]
    cond = jnp.concatenate([c, c_ctx[None], jnp.zeros((ADA_ROWS - nb - 1, D_MODEL), F32)], axis=0)
    mods = _ada_mods(cond, ada_w, ada_b).reshape(DEPTH, ADA_ROWS, 6, 1, D_MODEL)
    mods_sample = [[mods[i, :nb, m] for m in range(6)] for i in range(DEPTH)]
    mods_prompt = [[mods[i, nb:nb + 1, m] for m in range(6)] for i in range(DEPTH)]
    wts = (norm_mix, norm_ffn, attn_w_qkv.astype(BF16), attn_q_norm, attn_k_norm, attn_sink,
           attn_w_o.astype(BF16), pool_w.astype(BF16), pool_scale, fnet_w.astype(BF16),
           ffn_w_in.astype(BF16), ffn_conv_w, ffn_conv_b, ffn_w_out.astype(BF16))
    y_prompt, ks, vs = _trunk(x_prompt, mods_prompt, None, None, wts, tm=256)
    past = cache_k.shape[2]
    ctx_k = [cache_k[:, j].reshape(nb, past, KV_DIM).astype(BF16) for j in range(cache_k.shape[1])]
    ctx_v = [cache_v[:, j].reshape(nb, past, KV_DIM).astype(BF16).transpose(0, 2, 1) for j in range(cache_v.shape[1])]
    y_sample, _, _ = _trunk(x_sample, mods_sample, ctx_k, ctx_v, wts, tm=512)
    return (y_prompt, y_sample, jnp.stack(ks, axis=1), jnp.stack(vs, axis=1))
```

```python
import functools
import math

import jax
import jax.numpy as jnp
import numpy as np
from jax import lax
from jax.experimental import pallas as pl
from jax.experimental.pallas import tpu as pltpu

D_MODEL = 1024
DEPTH = 4
GRID_W = 64
N_MIXERS = 3
N_HEADS = 16
N_KV_HEADS = 4
HEAD_DIM = D_MODEL // N_HEADS
Q_PER_KV = N_HEADS // N_KV_HEADS
KV_DIM = N_KV_HEADS * HEAD_DIM
QKV_DIM = (N_HEADS + 2 * N_KV_HEADS) * HEAD_DIM
WINDOW = 128
BLOCK = 128
ROPE_THETA = 10000.0
POOL_WINDOWS = (2, 4, 8, 16)
N_GROUPS = 4
GROUP_DIM = D_MODEL // N_GROUPS
D_FF = ((8 * D_MODEL // 3 + 127) // 128) * 128
EPS = 1e-6
NEG_INF = -1e30
LOG2E = math.log2(math.e)

V7X_LANES = 128
V7X_SUBLANES = 8
V7X_BF16_ROWS = 16
V7X_MXU_DIM = 256
V7X_VMEM_BYTES = 64 * 1024 * 1024

F32 = jnp.float32
BF16 = jnp.bfloat16


def _vmem_limit(nbytes):
    return int(min(V7X_VMEM_BYTES - (4 << 20), nbytes * 3 // 2 + (4 << 20)))


def _silu(x):
    return x * (1.0 / (1.0 + jnp.exp(-x)))


def _split_bf16(a):
    hi = a.astype(BF16)
    lo = (a - hi.astype(F32)).astype(BF16)
    return hi, lo


def _norm_mod(x, gs, sh):
    ms = jnp.mean(x * x, axis=-1, keepdims=True)
    return x * lax.rsqrt(ms + EPS) * gs + sh


def _dot(a, b):
    return jnp.dot(a, b, preferred_element_type=F32)


ADA_ROWS = 16
ADA_TN = 1536


def _ada_kernel(c_ref, w_ref, b_ref, o_ref):
    s_hi, s_lo = _split_bf16(_silu(c_ref[...]))
    w_hi, w_lo = _split_bf16(w_ref[0])
    acc = _dot(s_hi, w_hi) + (_dot(s_lo, w_hi) + _dot(s_hi, w_lo))
    o_ref[0] = acc + b_ref[0]


def _ada_mods(cond, ada_w, ada_b):
    n_out = 6 * D_MODEL
    blk = D_MODEL * ADA_TN * 4
    return pl.pallas_call(
        _ada_kernel,
        grid=(DEPTH, n_out // ADA_TN),
        in_specs=[
            pl.BlockSpec((ADA_ROWS, D_MODEL), lambda i, j: (0, 0)),
            pl.BlockSpec((1, D_MODEL, ADA_TN), lambda i, j: (i, 0, j)),
            pl.BlockSpec((1, 1, ADA_TN), lambda i, j: (i, 0, j)),
        ],
        out_specs=pl.BlockSpec((1, ADA_ROWS, ADA_TN), lambda i, j: (i, 0, j)),
        out_shape=jax.ShapeDtypeStruct((DEPTH, ADA_ROWS, n_out), F32),
        compiler_params=pltpu.CompilerParams(
            dimension_semantics=("arbitrary", "arbitrary"),
            vmem_limit_bytes=_vmem_limit(2 * blk + 2 * blk),
        ),
        name="ada_mods",
    )(cond, ada_w, ada_b.reshape(DEPTH, 1, n_out))


FFN_CN = 256
FFN_HALO = 8
FFN_RC = 128


def _shift_rows(u, edge_row, down):
    n = u.shape[0]
    rid = lax.broadcasted_iota(jnp.int32, (V7X_SUBLANES, u.shape[1]), 0)
    if down:
        r = pltpu.roll(u, 1, 0)
        first = jnp.where(rid == 0, edge_row, r[:V7X_SUBLANES])
        return jnp.concatenate([first, r[V7X_SUBLANES:]], axis=0)
    r = pltpu.roll(u, n - 1, 0)
    last = jnp.where(rid == V7X_SUBLANES - 1, edge_row, r[n - V7X_SUBLANES:])
    return jnp.concatenate([r[:n - V7X_SUBLANES], last], axis=0)


def _ffn_kernel(xm_ref, xp_ref, xn_ref, g_ref, sh_ref, sc_ref, gate_ref, win_ref, cw_ref, cb_ref,
                wout_ref, o_ref, h_ref, a_ref, *, tm, nt):
    t = pl.program_id(1)
    gs = g_ref[...] * (1.0 + sc_ref[0])
    sh = sh_ref[0]

    for r in range(0, tm, FFN_RC):
        h_ref[r:r + FFN_RC, :] = _norm_mod(xm_ref[0, r:r + FFN_RC, :], gs, sh).astype(BF16)
    hp = _norm_mod(xp_ref[0], gs, sh) * (t > 0).astype(F32)
    hn = _norm_mod(xn_ref[0], gs, sh) * (t < nt - 1).astype(F32)
    h_ref[tm:tm + 2 * FFN_HALO, :] = jnp.concatenate([hp, hn], axis=0).astype(BF16)

    def conv(u, c0):
        um = u[:tm]
        prev_row = u[tm + FFN_HALO - 1:tm + FFN_HALO]
        next_row = u[tm + FFN_HALO:tm + FFN_HALO + 1]
        w = cw_ref[:, c0:c0 + FFN_CN]
        return (_shift_rows(um, prev_row, True) * w[0:1] + um * w[1:2]
                + _shift_rows(um, next_row, False) * w[2:3] + cb_ref[:, c0:c0 + FFN_CN])

    for j in range(D_FF // FFN_CN):
        c0 = j * FFN_CN
        hs = h_ref[...]
        gate = conv(_dot(hs, win_ref[:, c0:c0 + FFN_CN]), c0)
        val = conv(_dot(hs, win_ref[:, D_FF + c0:D_FF + c0 + FFN_CN]), D_FF + c0)
        a_ref[:, c0:c0 + FFN_CN] = (_silu(gate) * val).astype(BF16)

    o_ref[0] = xm_ref[0] + gate_ref[0] * _dot(a_ref[...], wout_ref[...])


def _ffn_layer(x, g, sh, sc, gate, w_in, conv_w, conv_b, w_out, tm):
    B, L, D = x.shape
    nt = L // tm
    bm = sh.shape[0]
    hb = tm // FFN_HALO
    mod_spec = pl.BlockSpec((1, 1, D), (lambda b, t: (b, 0, 0)) if bm > 1 else (lambda b, t: (0, 0, 0)))
    const = lambda b, t: (0, 0)
    est = (2 * D * 2 * D_FF * 2 + 2 * D_FF * D * 2 + 4 * tm * D * 4
           + (tm + 16) * D * 2 + tm * D_FF * 2 + 6 * (tm + 16) * FFN_CN * 4)
    return pl.pallas_call(
        functools.partial(_ffn_kernel, tm=tm, nt=nt),
        grid=(B, nt),
        in_specs=[
            pl.BlockSpec((1, tm, D), lambda b, t: (b, t, 0)),
            pl.BlockSpec((1, FFN_HALO, D), lambda b, t: (b, jnp.maximum(t * hb - 1, 0), 0)),
            pl.BlockSpec((1, FFN_HALO, D), lambda b, t: (b, jnp.minimum((t + 1) * hb, L // FFN_HALO - 1), 0)),
            pl.BlockSpec((1, D), const),
            mod_spec, mod_spec, mod_spec,
            pl.BlockSpec((D, 2 * D_FF), const),
            pl.BlockSpec((3, 2 * D_FF), const),
            pl.BlockSpec((1, 2 * D_FF), const),
            pl.BlockSpec((D_FF, D), const),
        ],
        out_specs=pl.BlockSpec((1, tm, D), lambda b, t: (b, t, 0)),
        out_shape=jax.ShapeDtypeStruct((B, L, D), F32),
        scratch_shapes=[
            pltpu.VMEM((tm + 2 * FFN_HALO, D), BF16),
            pltpu.VMEM((tm, D_FF), BF16),
        ],
        compiler_params=pltpu.CompilerParams(
            dimension_semantics=("arbitrary", "arbitrary"),
            vmem_limit_bytes=_vmem_limit(est),
        ),
        name="conv_ffn",
    )(x, x, x, g.reshape(1, D), sh, sc, gate, w_in, conv_w, conv_b.reshape(1, 2 * D_FF), w_out)


def _mod_spec(bm):
    return pl.BlockSpec((1, 1, D_MODEL), (lambda b, t: (b, 0, 0)) if bm > 1 else (lambda b, t: (0, 0, 0)))


_CONST2 = lambda b, t: (0, 0)


def _proj_res_kernel(x_ref, a_ref, w_ref, gate_ref, o_ref):
    o_ref[0] = x_ref[0] + gate_ref[0] * _dot(a_ref[0], w_ref[...])


def _proj_res(x, a, w, gate, tm):
    B, L, D = x.shape
    kdim = a.shape[-1]
    est = 2 * kdim * D * 2 + 4 * tm * D * 4 + 2 * tm * kdim * 2 + tm * D * 4
    return pl.pallas_call(
        _proj_res_kernel,
        grid=(B, L // tm),
        in_specs=[
            pl.BlockSpec((1, tm, D), lambda b, t: (b, t, 0)),
            pl.BlockSpec((1, tm, kdim), lambda b, t: (b, t, 0)),
            pl.BlockSpec((kdim, D), _CONST2),
            _mod_spec(gate.shape[0]),
        ],
        out_specs=pl.BlockSpec((1, tm, D), lambda b, t: (b, t, 0)),
        out_shape=jax.ShapeDtypeStruct((B, L, D), F32),
        compiler_params=pltpu.CompilerParams(
            dimension_semantics=("arbitrary", "arbitrary"), vmem_limit_bytes=_vmem_limit(est)),
        name="proj_res",
    )(x, a, w, gate)


def _head_rms(y, e, et, width):
    s_hi, s_lo = _split_bf16(y * y)
    ss = _dot(s_hi, e[:width]) + _dot(s_lo, e[:width])
    r_hi, r_lo = _split_bf16(lax.rsqrt(ss * (1.0 / HEAD_DIM) + EPS))
    return _dot(r_hi, et[:, :width]) + _dot(r_lo, et[:, :width])


def _rope(y, cos, sin_signed, first):
    outs = []
    for c0 in range(0, y.shape[1], V7X_LANES):
        ys = y[:, c0:c0 + V7X_LANES]
        partner = jnp.where(first, pltpu.roll(ys, V7X_LANES - 16, 1), pltpu.roll(ys, 16, 1))
        outs.append(ys * cos + partner * sin_signed)
    return jnp.concatenate(outs, axis=1)


def _qkv_kernel(*refs, rope, keep_f32):
    x_ref, g_ref, sh_ref, sc_ref, w_ref, qn_ref, kn_ref, e_ref, et_ref = refs[:9]
    refs = refs[9:]
    if rope:
        cos_ref, sin_ref, cost_ref, sint_ref = refs[:4]
        refs = refs[4:]
    q_ref, k_ref, v_ref = refs[:3]
    gs = g_ref[...] * (1.0 + sc_ref[0])
    h = _norm_mod(x_ref[0], gs, sh_ref[0]).astype(BF16)
    qkv = _dot(h, w_ref[...])
    k = qkv[:, D_MODEL:D_MODEL + KV_DIM]
    v = qkv[:, D_MODEL + KV_DIM:]
    k = k * _head_rms(k, e_ref[...], et_ref[...], KV_DIM) * kn_ref[...]
    if keep_f32:
        refs[3][0] = k
        refs[4][0] = v
    if rope:
        lane = lax.broadcasted_iota(jnp.int32, (x_ref.shape[1], V7X_LANES), 1)
        k = _rope(k, cos_ref[...], sin_ref[...], (lane % 32) < 16)
    k_ref[0] = k.astype(BF16)
    _put_blocks(v_ref, 0, v.T.astype(BF16))
    qt = qkv[:, :D_MODEL].T
    quarter = HEAD_DIM // 4
    for hd in range(N_HEADS):
        qh = qt[hd * HEAD_DIM:(hd + 1) * HEAD_DIM]
        ms = jnp.sum(qh * qh, axis=0, keepdims=True) * (1.0 / HEAD_DIM)
        qh = qh * lax.rsqrt(ms + EPS) * qn_ref[...]
        if rope:
            parts = [qh[i * quarter:(i + 1) * quarter] for i in range(4)]
            partner = jnp.concatenate([parts[1], parts[0], parts[3], parts[2]], axis=0)
            qh = qh * cost_ref[...] + partner * sint_ref[...]
        _put_blocks(q_ref, hd * HEAD_DIM, qh.astype(BF16))


def _put_blocks(ref, row0, val):
    for i in range(ref.shape[1]):
        ref[0, i, row0:row0 + val.shape[0], :] = val[:, i * BLOCK:(i + 1) * BLOCK]


def _head_selectors():
    col = np.arange(D_MODEL)[:, None] // HEAD_DIM
    e = (col == np.arange(V7X_LANES)[None, :]).astype(np.float32)
    return jnp.asarray(e, BF16), jnp.asarray(e.T, BF16)


def _rope_tables(L):
    half = HEAD_DIM // 2
    rows = jnp.repeat(jnp.arange(L // GRID_W), GRID_W)
    cols = jnp.tile(jnp.arange(GRID_W), L // GRID_W)
    inv_freq = 1.0 / (ROPE_THETA ** (jnp.arange(0, half, 2, dtype=F32) / half))
    ar = rows.astype(F32)[:, None] * inv_freq
    ac = cols.astype(F32)[:, None] * inv_freq
    cos = jnp.concatenate([jnp.cos(ar), jnp.cos(ar), jnp.cos(ac), jnp.cos(ac)], axis=1)
    sin = jnp.concatenate([-jnp.sin(ar), jnp.sin(ar), -jnp.sin(ac), jnp.sin(ac)], axis=1)
    return jnp.tile(cos, (1, 2)), jnp.tile(sin, (1, 2)), cos.T, sin.T


def _qkv_layer(x, g, sh, sc, w_qkv, q_norm, k_norm, tm, rope):
    B, L, D = x.shape
    keep_f32 = not rope
    e, et = _head_selectors()
    qn = jnp.broadcast_to((q_norm * (HEAD_DIM ** -0.5 * LOG2E))[:, None], (HEAD_DIM, tm))
    kn = jnp.tile(k_norm, N_KV_HEADS).reshape(1, KV_DIM)
    ms = _mod_spec(sh.shape[0])
    in_specs = [
        pl.BlockSpec((1, tm, D), lambda b, t: (b, t, 0)),
        pl.BlockSpec((1, D), _CONST2), ms, ms,
        pl.BlockSpec((D, QKV_DIM), _CONST2),
        pl.BlockSpec((HEAD_DIM, tm), _CONST2),
        pl.BlockSpec((1, KV_DIM), _CONST2),
        pl.BlockSpec((KV_DIM, V7X_LANES), _CONST2),
        pl.BlockSpec((V7X_LANES, KV_DIM), _CONST2),
    ]
    args = [x, g.reshape(1, D), sh, sc, w_qkv, qn, kn, e[:KV_DIM], et[:, :KV_DIM]]
    if rope:
        cos, sin, cos_t, sin_t = _rope_tables(L)
        in_specs += [pl.BlockSpec((tm, V7X_LANES), lambda b, t: (t, 0))] * 2
        in_specs += [pl.BlockSpec((HEAD_DIM, tm), lambda b, t: (0, t))] * 2
        args += [cos, sin, cos_t, sin_t]
    tok = lambda w: pl.BlockSpec((1, tm, w), lambda b, t: (b, t, 0))
    tok_t = lambda w: pl.BlockSpec((1, tm // BLOCK, w, BLOCK), lambda b, t: (b, t, 0, 0))
    out_specs = [tok_t(D), tok(KV_DIM), tok_t(KV_DIM)]
    out_shape = [jax.ShapeDtypeStruct((B, L // BLOCK, D, BLOCK), BF16), jax.ShapeDtypeStruct((B, L, KV_DIM), BF16),
                 jax.ShapeDtypeStruct((B, L // BLOCK, KV_DIM, BLOCK), BF16)]
    if keep_f32:
        out_specs += [tok(KV_DIM), tok(KV_DIM)]
        out_shape += [jax.ShapeDtypeStruct((B, L, KV_DIM), F32)] * 2
    est = 2 * D * QKV_DIM * 2 + 2 * tm * D * 4 + 10 * tm * QKV_DIM * 4
    return pl.pallas_call(
        functools.partial(_qkv_kernel, rope=rope, keep_f32=keep_f32),
        grid=(B, L // tm),
        in_specs=in_specs,
        out_specs=out_specs,
        out_shape=out_shape,
        compiler_params=pltpu.CompilerParams(
            dimension_semantics=("arbitrary", "arbitrary"), vmem_limit_bytes=_vmem_limit(est)),
        name="qkv_rope" if rope else "qkv",
    )(*args)


ONES_ROWS = V7X_BF16_ROWS


def _attend(qt_ref, parts, sink_ref, x_ref, gate_ref, wo_ref, o_ref, ot_ref):
    staged = []
    for kh in range(N_KV_HEADS):
        heads = [kh * Q_PER_KV + g for g in range(Q_PER_KV)]
        qt = jnp.concatenate([qt_ref[0, 0, h * HEAD_DIM:(h + 1) * HEAD_DIM, :] for h in heads], axis=1)
        sink = jnp.concatenate([jnp.full((1, BLOCK), sink_ref[h] * LOG2E, F32) for h in heads], axis=1)
        scores = []
        m = sink
        for k, _, bias in parts:
            s = _dot(k[:, kh * HEAD_DIM:(kh + 1) * HEAD_DIM], qt)
            if bias is not None:
                s = s + bias
            scores.append(s)
            m = jnp.maximum(m, jnp.max(s, axis=0, keepdims=True))
        staged.append((heads, sink, m, scores))
    for kh, (heads, sink, m, scores) in enumerate(staged):
        acc = jnp.zeros((HEAD_DIM + ONES_ROWS, Q_PER_KV * BLOCK), F32)
        for s, (_, vt, _) in zip(scores, parts):
            p = jnp.exp2(s - m).astype(BF16)
            vt_ext = jnp.concatenate([vt[kh * HEAD_DIM:(kh + 1) * HEAD_DIM],
                                      jnp.ones((ONES_ROWS, vt.shape[1]), BF16)], axis=0)
            acc = acc + _dot(vt_ext, p)
        denom = jnp.exp2(sink - m) + acc[HEAD_DIM:HEAD_DIM + 1]
        out = acc[:HEAD_DIM] / denom
        for g, h in enumerate(heads):
            ot_ref[h * HEAD_DIM:(h + 1) * HEAD_DIM, :] = out[:, g * BLOCK:(g + 1) * BLOCK]
    attn = ot_ref[...].T.astype(BF16)
    o_ref[0] = x_ref[0] + gate_ref[0] * _dot(attn, wo_ref[...])


def _attn_latent_kernel(sink_ref, qt_ref, kp_ref, kc_ref, kn_ref, vp_ref, vc_ref, vn_ref, ck_ref, cv_ref,
                        bias_ref, x_ref, gate_ref, wo_ref, o_ref, ot_ref):
    kw = jnp.concatenate([kp_ref[0], kc_ref[0], kn_ref[0]], axis=0)
    vw = jnp.concatenate([vp_ref[0, 0], vc_ref[0, 0], vn_ref[0, 0]], axis=1)
    _attend(qt_ref, [(kw, vw, bias_ref[0]), (ck_ref[0], cv_ref[0], None)], sink_ref,
            x_ref, gate_ref, wo_ref, o_ref, ot_ref)


def _attn_context_kernel(sink_ref, qt_ref, k_ref, v_ref, x_ref, gate_ref, wo_ref, o_ref, ot_ref):
    vt = jnp.concatenate([v_ref[0, i] for i in range(v_ref.shape[1])], axis=1)
    _attend(qt_ref, [(k_ref[0], vt, None)], sink_ref, x_ref, gate_ref, wo_ref, o_ref, ot_ref)


def _band_bias():
    sj = np.arange(3 * BLOCK)[:, None]
    qi = np.arange(Q_PER_KV * BLOCK)[None, :] % BLOCK
    band = np.abs(sj - BLOCK - qi) <= WINDOW
    first = band & (sj >= BLOCK)
    last = band & (sj < 2 * BLOCK)
    return jnp.asarray(np.where(np.stack([first, band, last]), 0.0, NEG_INF), F32)


def _attention(x, gate, w_o, qt, k, vt, sink, ctx_k=None, ctx_vt=None):
    B, nqb, D, _ = qt.shape
    L = nqb * BLOCK
    qspec = pl.BlockSpec((1, 1, D, BLOCK), lambda b, n: (b, n, 0, 0))
    xspec = pl.BlockSpec((1, BLOCK, D), lambda b, n: (b, n, 0))
    smem = pl.BlockSpec(memory_space=pltpu.SMEM)
    tail_specs = [xspec, _mod_spec(gate.shape[0]), pl.BlockSpec((D, D), _CONST2)]
    if ctx_k is None:
        kspec = pl.BlockSpec((1, L, KV_DIM), lambda b, n: (b, 0, 0))
        vspec = pl.BlockSpec((1, nqb, KV_DIM, BLOCK), lambda b, n: (b, 0, 0, 0))
        body, in_specs, args = _attn_context_kernel, [smem, qspec, kspec, vspec], (sink, qt, k, vt)
    else:
        P = ctx_k.shape[1]
        prev = lambda n: jnp.maximum(n - 1, 0)
        nxt = lambda n: jnp.minimum(n + 1, nqb - 1)
        which = lambda n: jnp.where(n == 0, 0, jnp.where(n == nqb - 1, 2, 1))
        kspecs = [pl.BlockSpec((1, BLOCK, KV_DIM), lambda b, n, f=f: (b, f(n), 0)) for f in (prev, lambda n: n, nxt)]
        vspecs = [pl.BlockSpec((1, 1, KV_DIM, BLOCK), lambda b, n, f=f: (b, f(n), 0, 0))
                  for f in (prev, lambda n: n, nxt)]
        body = _attn_latent_kernel
        in_specs = [smem, qspec, *kspecs, *vspecs,
                    pl.BlockSpec((1, P, KV_DIM), lambda b, n: (b, 0, 0)),
                    pl.BlockSpec((1, KV_DIM, P), lambda b, n: (b, 0, 0)),
                    pl.BlockSpec((1, 3 * BLOCK, Q_PER_KV * BLOCK), lambda b, n: (which(n), 0, 0))]
        args = (sink, qt, k, k, k, vt, vt, vt, ctx_k, ctx_vt, _band_bias())
    return pl.pallas_call(
        body,
        grid=(B, nqb),
        in_specs=in_specs + tail_specs,
        out_specs=xspec,
        out_shape=jax.ShapeDtypeStruct((B, L, D), F32),
        scratch_shapes=[pltpu.VMEM((D, BLOCK), F32)],
        compiler_params=pltpu.CompilerParams(
            dimension_semantics=("arbitrary", "arbitrary"), vmem_limit_bytes=_vmem_limit(28 << 20)),
        name="attn_context" if ctx_k is None else "attn_latent",
    )(*args, x, gate, w_o)


POOL_TM = 256
POOL_HALO = 8


def _pool_kernel(xm_ref, xp_ref, xn_ref, g_ref, sh_ref, sc_ref, gate_ref, band_ref, pw_ref, ps_ref, o_ref,
                 *, nt, seq_len, n_sub):
    t = pl.program_id(1)
    gs = g_ref[...] * (1.0 + sc_ref[0])
    sh = sh_ref[0]
    xm = xm_ref[0]
    hm = _norm_mod(xm, gs, sh)
    hp = _norm_mod(xp_ref[0], gs, sh) * (t > 0).astype(F32)
    hn = _norm_mod(xn_ref[0], gs, sh) * (t < nt - 1).astype(F32)
    gate = gate_ref[0]
    for u in range(n_sub):
        r0 = u * POOL_TM
        rows = slice(r0, r0 + POOL_TM)
        before = hp if u == 0 else hm[r0 - POOL_HALO:r0]
        after = hn if u == n_sub - 1 else hm[r0 + POOL_TM:r0 + POOL_TM + POOL_HALO]
        h_hi, h_lo = _split_bf16(jnp.concatenate([before, hm[rows], after], axis=0))
        tpos = (t * n_sub + u) * POOL_TM + lax.broadcasted_iota(jnp.int32, (POOL_TM, GROUP_DIM), 0)
        for g, win in enumerate(POOL_WINDOWS):
            back = win // 2
            fwd = win - back - 1
            cs = slice(g * GROUP_DIM, (g + 1) * GROUP_DIM)
            band = band_ref[g]
            wsum = _dot(band, h_hi[:, cs]) + _dot(band, h_lo[:, cs])
            cnt = jnp.minimum(tpos + fwd + 1, seq_len) - jnp.maximum(tpos - back, 0)
            pooled = wsum / cnt.astype(F32) - hm[rows, cs]
            mixed = _dot(pooled.astype(BF16), pw_ref[g])
            o_ref[0, rows, cs] = xm[rows, cs] + gate[:, cs] * (mixed * ps_ref[:, cs])


def _pool_bands():
    r = np.arange(POOL_TM)[:, None] + POOL_HALO
    c = np.arange(POOL_TM + 2 * POOL_HALO)[None, :]
    bands = [((c >= r - w // 2) & (c <= r + (w - w // 2 - 1))) for w in POOL_WINDOWS]
    return jnp.asarray(np.stack(bands).astype(np.float32), BF16)


def _pool_layer(x, g, sh, sc, gate, pool_w, pool_scale):
    B, L, D = x.shape
    n_sub = 2 if L % (2 * POOL_TM) == 0 else 1
    tm = n_sub * POOL_TM
    nt = L // tm
    hb = tm // POOL_HALO
    ms = _mod_spec(sh.shape[0])
    ext = POOL_TM + 2 * POOL_HALO
    return pl.pallas_call(
        functools.partial(_pool_kernel, nt=nt, seq_len=L, n_sub=n_sub),
        grid=(B, nt),
        in_specs=[
            pl.BlockSpec((1, tm, D), lambda b, t: (b, t, 0)),
            pl.BlockSpec((1, POOL_HALO, D), lambda b, t: (b, jnp.maximum(t * hb - 1, 0), 0)),
            pl.BlockSpec((1, POOL_HALO, D), lambda b, t: (b, jnp.minimum((t + 1) * hb, L // POOL_HALO - 1), 0)),
            pl.BlockSpec((1, D), _CONST2), ms, ms, ms,
            pl.BlockSpec((N_GROUPS, POOL_TM, ext), lambda b, t: (0, 0, 0)),
            pl.BlockSpec((N_GROUPS, GROUP_DIM, GROUP_DIM), lambda b, t: (0, 0, 0)),
            pl.BlockSpec((1, D), _CONST2),
        ],
        out_specs=pl.BlockSpec((1, tm, D), lambda b, t: (b, t, 0)),
        out_shape=jax.ShapeDtypeStruct((B, L, D), F32),
        compiler_params=pltpu.CompilerParams(
            dimension_semantics=("arbitrary", "arbitrary"), vmem_limit_bytes=_vmem_limit(24 << 20)),
        name="pool_mix",
    )(x, x, x, g.reshape(1, D), sh, sc, gate, _pool_bands(), pool_w, pool_scale.reshape(1, D))


def _dft_tables(n):
    k = np.arange(n)
    ang = 2.0 * np.pi * ((k[:, None] * k[None, :]) % n) / n
    return np.cos(ang), np.sin(ang)


def _bf16_table(a):
    return jnp.asarray(a, F32).astype(BF16)


def _chan_dft_kernel(x_ref, g_ref, sh_ref, sc_ref, cc_ref, sc_tab_ref, y_ref):
    gs = g_ref[...] * (1.0 + sc_ref[0])
    h = _norm_mod(x_ref[0], gs, sh_ref[0]).astype(BF16)
    for g in range(N_GROUPS):
        cs = slice(g * GROUP_DIM, (g + 1) * GROUP_DIM)
        y_ref[0, 0, :, cs] = _dot(h[:, cs], cc_ref[...]).astype(BF16)
        y_ref[0, 1, :, cs] = _dot(h[:, cs], sc_tab_ref[...]).astype(BF16)


def _chan_dft(x, g, sh, sc, tm):
    B, L, D = x.shape
    cc, sn = _dft_tables(GROUP_DIM)
    ms = _mod_spec(sh.shape[0])
    tab = pl.BlockSpec((GROUP_DIM, GROUP_DIM), _CONST2)
    return pl.pallas_call(
        _chan_dft_kernel,
        grid=(B, L // tm),
        in_specs=[pl.BlockSpec((1, tm, D), lambda b, t: (b, t, 0)), pl.BlockSpec((1, D), _CONST2), ms, ms, tab, tab],
        out_specs=pl.BlockSpec((1, 2, tm, D), lambda b, t: (b, 0, t, 0)),
        out_shape=jax.ShapeDtypeStruct((B, 2, L, D), BF16),
        compiler_params=pltpu.CompilerParams(
            dimension_semantics=("arbitrary", "arbitrary"), vmem_limit_bytes=_vmem_limit(12 * tm * D * 4)),
        name="chan_dft",
    )(x, g.reshape(1, D), sh, sc, _bf16_table(cc), _bf16_table(sn))


def _pos_dft_kernel(w_ref, y_ref, o_ref, *, scale):
    o_ref[0] = (_dot(w_ref[...], y_ref[0]) * scale).astype(BF16)


def _pos_dft(y, L, tmo, tn):
    B, _, D = y.shape
    cl, sl = _dft_tables(L)
    w = _bf16_table(np.concatenate([cl, -sl], axis=1))
    scale = 1.0 / math.sqrt(L * GROUP_DIM)
    est = 2 * tmo * 2 * L * 2 + 2 * 2 * L * tn * 2 + 3 * tmo * tn * 4
    return pl.pallas_call(
        functools.partial(_pos_dft_kernel, scale=scale),
        grid=(B, D // tn, L // tmo),
        in_specs=[
            pl.BlockSpec((tmo, 2 * L), lambda b, j, i: (i, 0)),
            pl.BlockSpec((1, 2 * L, tn), lambda b, j, i: (b, 0, j)),
        ],
        out_specs=pl.BlockSpec((1, tmo, tn), lambda b, j, i: (b, i, j)),
        out_shape=jax.ShapeDtypeStruct((B, L, D), BF16),
        compiler_params=pltpu.CompilerParams(
            dimension_semantics=("arbitrary", "arbitrary", "arbitrary"), vmem_limit_bytes=_vmem_limit(est)),
        name="pos_dft",
    )(w, y)


FFT_N = 64
FFT_T = 16
FFT_PITCH = 24


def _fft_stage1_kernel(x_ref, g_ref, sh_ref, sc_ref, cc_ref, ns_ref, tw_ref, y_ref, gbuf_ref):
    gs = g_ref[...] * (1.0 + sc_ref[0])
    h = _norm_mod(x_ref[0].reshape(FFT_N * FFT_T, D_MODEL), gs, sh_ref[0]).astype(BF16)
    for g in range(N_GROUPS):
        cs = slice(g * GROUP_DIM, (g + 1) * GROUP_DIM)
        _put_groups(gbuf_ref.at[0], g * GROUP_DIM, _dot(h[:, cs], cc_ref[...]))
        _put_groups(gbuf_ref.at[1], g * GROUP_DIM, _dot(h[:, cs], ns_ref[...]))
    for t in range(FFT_T):
        sel = pl.ds(t, FFT_N, stride=FFT_PITCH)
        gst = jnp.concatenate([_get_cols(gbuf_ref.at[0], sel), _get_cols(gbuf_ref.at[1], sel)], axis=0)
        y = _dot(tw_ref[t], gst.astype(BF16))
        _put_cols(gbuf_ref.at[0], sel, y[:FFT_N])
        _put_cols(gbuf_ref.at[1], sel, y[FFT_N:])
    for p in range(2):
        y_ref[0, p] = _get_groups(gbuf_ref.at[p]).reshape(FFT_N, FFT_T, D_MODEL).astype(BF16)


def _get_cols(buf_ref, rows):
    return jnp.concatenate([buf_ref[c, rows, :] for c in range(buf_ref.shape[0])], axis=1)


def _put_cols(buf_ref, rows, val):
    for c in range(buf_ref.shape[0]):
        buf_ref[c, rows, :] = val[:, c * V7X_LANES:(c + 1) * V7X_LANES]


def _get_groups(buf_ref):
    val = _get_cols(buf_ref, slice(None))
    return val.reshape(FFT_N, FFT_PITCH, val.shape[1])[:, :FFT_T].reshape(FFT_N * FFT_T, val.shape[1])


def _put_groups(buf_ref, col0, val):
    pad = jnp.zeros((FFT_N, FFT_PITCH - FFT_T, val.shape[1]), val.dtype)
    val = jnp.concatenate([val.reshape(FFT_N, FFT_T, val.shape[1]), pad], axis=1)
    val = val.reshape(FFT_N * FFT_PITCH, val.shape[2])
    for c in range(val.shape[1] // V7X_LANES):
        buf_ref[col0 // V7X_LANES + c] = val[:, c * V7X_LANES:(c + 1) * V7X_LANES]


def _fft_stage2_kernel(y_ref, x_ref, w_ref, gate_ref, r_ref, o_ref, zbuf_ref, *, scale):
    @pl.when((pl.program_id(0) == 0) & (pl.program_id(1) == 0))
    def _():
        zbuf_ref[...] = jnp.zeros(zbuf_ref.shape, F32)

    for f in range(FFT_T):
        yst = jnp.concatenate([y_ref[0, 0, f], y_ref[0, 1, f]], axis=0)
        _put_cols(zbuf_ref, pl.ds(f, FFT_N, stride=FFT_PITCH), _dot(r_ref[...], yst) * scale)
    mix = _dot(_get_groups(zbuf_ref).astype(BF16), w_ref[...])
    x = x_ref[0].reshape(FFT_N * FFT_T, D_MODEL)
    o_ref[0] = (x + gate_ref[0] * mix).reshape(FFT_N, FFT_T, D_MODEL)


def _fnet_fft_layer(x, g, sh, sc, gate, fnet_w):
    B, L, D = x.shape
    n, tt = FFT_N, FFT_T
    x4 = x.reshape(B, n, n, D)
    cc, sn = _dft_tables(GROUP_DIM)
    f2 = np.arange(n)[None, :, None]
    tpos = np.arange(n)[:, None, None] + n * np.arange(n)[None, None, :]
    ang = 2.0 * np.pi * ((f2 * tpos) % L) / L
    c1, s1 = np.cos(ang), np.sin(ang)
    tw = np.concatenate([np.concatenate([c1, s1], axis=2), np.concatenate([-s1, c1], axis=2)], axis=1)
    c2, s2 = _dft_tables(n)
    r = np.concatenate([c2, s2], axis=1)
    ms = _mod_spec(sh.shape[0])
    tab = pl.BlockSpec((GROUP_DIM, GROUP_DIM), _CONST2)
    y = pl.pallas_call(
        _fft_stage1_kernel,
        grid=(B, n // tt),
        in_specs=[
            pl.BlockSpec((1, n, tt, D), lambda b, i: (b, 0, i, 0)),
            pl.BlockSpec((1, D), _CONST2), ms, ms, tab, tab,
            pl.BlockSpec((tt, 2 * n, 2 * n), lambda b, i: (i, 0, 0)),
        ],
        out_specs=pl.BlockSpec((1, 2, n, tt, D), lambda b, i: (b, 0, 0, i, 0)),
        out_shape=jax.ShapeDtypeStruct((B, 2, n, n, D), BF16),
        scratch_shapes=[pltpu.VMEM((2, D // V7X_LANES, n * FFT_PITCH, V7X_LANES), F32)],
        compiler_params=pltpu.CompilerParams(
            dimension_semantics=("arbitrary", "arbitrary"),
            vmem_limit_bytes=_vmem_limit(2 * n * tt * D * (4 + 4 + 4) + 2 * n * tt * D * 4)),
        name="fft_stage1",
    )(x4, g.reshape(1, D), sh, sc, _bf16_table(cc), _bf16_table(-sn), _bf16_table(tw))
    out = pl.pallas_call(
        functools.partial(_fft_stage2_kernel, scale=1.0 / math.sqrt(L * GROUP_DIM)),
        grid=(B, n // tt),
        in_specs=[
            pl.BlockSpec((1, 2, tt, n, D), lambda b, j: (b, 0, j, 0, 0)),
            pl.BlockSpec((1, n, tt, D), lambda b, j: (b, 0, j, 0)),
            pl.BlockSpec((D, D), _CONST2),
            _mod_spec(gate.shape[0]),
            pl.BlockSpec((n, 2 * n), _CONST2),
        ],
        out_specs=pl.BlockSpec((1, n, tt, D), lambda b, j: (b, 0, j, 0)),
        out_shape=jax.ShapeDtypeStruct((B, n, n, D), F32),
        scratch_shapes=[pltpu.VMEM((D // V7X_LANES, n * FFT_PITCH, V7X_LANES), F32)],
        compiler_params=pltpu.CompilerParams(
            dimension_semantics=("arbitrary", "arbitrary"),
            vmem_limit_bytes=_vmem_limit(n * tt * D * (4 + 8 + 8 + 4 + 4) + 2 * D * D * 2)),
        name="fft_stage2",
    )(y, x4, fnet_w, gate, _bf16_table(r))
    return out.reshape(B, L, D)


def _trunk(x, mods, ctx_k, ctx_v, wts, tm):
    (norm_mix, norm_ffn, w_qkv, q_norm, k_norm, sink, w_o, pool_w, pool_scale, fnet_w,
     w_in, conv_w, conv_b, w_out) = wts
    B, L, D = x.shape
    latent = ctx_k is not None
    ks, vs = [], []
    for i in range(DEPTH):
        kind, j = i % N_MIXERS, i // N_MIXERS
        sh1, sc1, g1, sh2, sc2, g2 = mods[i]
        if kind == 0:
            outs = _qkv_layer(x, norm_mix[i], sh1, sc1, w_qkv[j], q_norm[j], k_norm[j], tm, rope=latent)
            q, k, v = outs[:3]
            if latent:
                x = _attention(x, g1, w_o[j], q, k, v, sink[j], ctx_k[j], ctx_v[j])
            else:
                x = _attention(x, g1, w_o[j], q, k, v, sink[j])
                ks.append(outs[3].reshape(B, L, N_KV_HEADS, HEAD_DIM))
                vs.append(outs[4].reshape(B, L, N_KV_HEADS, HEAD_DIM))
        elif kind == 1:
            x = _pool_layer(x, norm_mix[i], sh1, sc1, g1, pool_w[j], pool_scale[j])
        elif L == FFT_N * FFT_N:
            x = _fnet_fft_layer(x, norm_mix[i], sh1, sc1, g1, fnet_w[j])
        else:
            y = _chan_dft(x, norm_mix[i], sh1, sc1, tm).reshape(B, 2 * L, D)
            f = _pos_dft(y, L, min(L, 512), 512 if L > 512 else D)
            x = _proj_res(x, f, fnet_w[j], g1, tm)
        x = _ffn_layer(x, norm_ffn[i], sh2, sc2, g2, w_in[i], conv_w[i], conv_b[i], w_out[i], tm)
    return x, ks, vs


def kernel(x_prompt, x_sample, cache_k, cache_v, c, c_ctx, norm_mix, norm_ffn, ada_w, ada_b, attn_w_qkv,
           attn_q_norm, attn_k_norm, attn_sink, attn_w_o, pool_w, pool_scale, fnet_w, ffn_w_in, ffn_conv_w,
           ffn_conv_b, ffn_w_out):
    nb = c.shape[0]
    cond = jnp.concatenate([c, c_ctx[None], jnp.zeros((ADA_ROWS - nb - 1, D_MODEL), F32)], axis=0)
    mods = _ada_mods(cond, ada_w, ada_b).reshape(DEPTH, ADA_ROWS, 6, 1, D_MODEL)
    mods_sample = [[mods[i, :nb, m] for m in range(6)] for i in range(DEPTH)]
    mods_prompt = [[mods[i, nb:nb + 1, m] for m in range(6)] for i in range(DEPTH)]
    wts = (norm_mix, norm_ffn, attn_w_qkv.astype(BF16), attn_q_norm, attn_k_norm, attn_sink,
           attn_w_o.astype(BF16), pool_w.astype(BF16), pool_scale, fnet_w.astype(BF16),
           ffn_w_in.astype(BF16), ffn_conv_w, ffn_conv_b, ffn_w_out.astype(BF16))
    y_prompt, ks, vs = _trunk(x_prompt, mods_prompt, None, None, wts, tm=256)
    past = cache_k.shape[2]
    ctx_k = [cache_k[:, j].reshape(nb, past, KV_DIM).astype(BF16) for j in range(cache_k.shape[1])]
    ctx_v = [cache_v[:, j].reshape(nb, past, KV_DIM).astype(BF16).transpose(0, 2, 1) for j in range(cache_v.shape[1])]
    y_sample, _, _ = _trunk(x_sample, mods_sample, ctx_k, ctx_v, wts, tm=512)
    return (y_prompt, y_sample, jnp.stack(ks, axis=1), jnp.stack(vs, axis=1))
```

```python
import functools
import math

import jax
import jax.numpy as jnp
import numpy as np
from jax import lax
from jax.experimental import pallas as pl
from jax.experimental.pallas import tpu as pltpu

D_MODEL = 1024
DEPTH = 4
GRID_W = 64
N_MIXERS = 3
N_HEADS = 16
N_KV_HEADS = 4
HEAD_DIM = D_MODEL // N_HEADS
Q_PER_KV = N_HEADS // N_KV_HEADS
KV_DIM = N_KV_HEADS * HEAD_DIM
QKV_DIM = (N_HEADS + 2 * N_KV_HEADS) * HEAD_DIM
WINDOW = 128
BLOCK = 128
ROPE_THETA = 10000.0
POOL_WINDOWS = (2, 4, 8, 16)
N_GROUPS = 4
GROUP_DIM = D_MODEL // N_GROUPS
D_FF = ((8 * D_MODEL // 3 + 127) // 128) * 128
EPS = 1e-6
NEG_INF = -1e30
LOG2E = math.log2(math.e)

V7X_LANES = 128
V7X_SUBLANES = 8
V7X_BF16_ROWS = 16
V7X_MXU_DIM = 256
V7X_VMEM_BYTES = 64 * 1024 * 1024

F32 = jnp.float32
BF16 = jnp.bfloat16


def _vmem_limit(nbytes):
    return int(min(V7X_VMEM_BYTES - (4 << 20), nbytes * 3 // 2 + (4 << 20)))


def _silu(x):
    return x * (1.0 / (1.0 + jnp.exp(-x)))


def _split_bf16(a):
    hi = a.astype(BF16)
    lo = (a - hi.astype(F32)).astype(BF16)
    return hi, lo


def _norm_mod(x, gs, sh):
    ms = jnp.mean(x * x, axis=-1, keepdims=True)
    return x * lax.rsqrt(ms + EPS) * gs + sh


def _dot(a, b):
    return jnp.dot(a, b, preferred_element_type=F32)


ADA_ROWS = 16
ADA_TN = 1536


def _ada_kernel(c_ref, w_ref, b_ref, o_ref):
    s_hi, s_lo = _split_bf16(_silu(c_ref[...]))
    w_hi, w_lo = _split_bf16(w_ref[0])
    acc = _dot(s_hi, w_hi) + (_dot(s_lo, w_hi) + _dot(s_hi, w_lo))
    o_ref[0] = acc + b_ref[0]


def _ada_mods(cond, ada_w, ada_b):
    n_out = 6 * D_MODEL
    blk = D_MODEL * ADA_TN * 4
    return pl.pallas_call(
        _ada_kernel,
        grid=(DEPTH, n_out // ADA_TN),
        in_specs=[
            pl.BlockSpec((ADA_ROWS, D_MODEL), lambda i, j: (0, 0)),
            pl.BlockSpec((1, D_MODEL, ADA_TN), lambda i, j: (i, 0, j)),
            pl.BlockSpec((1, 1, ADA_TN), lambda i, j: (i, 0, j)),
        ],
        out_specs=pl.BlockSpec((1, ADA_ROWS, ADA_TN), lambda i, j: (i, 0, j)),
        out_shape=jax.ShapeDtypeStruct((DEPTH, ADA_ROWS, n_out), F32),
        compiler_params=pltpu.CompilerParams(
            dimension_semantics=("arbitrary", "arbitrary"),
            vmem_limit_bytes=_vmem_limit(2 * blk + 2 * blk),
        ),
        name="ada_mods",
    )(cond, ada_w, ada_b.reshape(DEPTH, 1, n_out))


FFN_CN = 256
FFN_HALO = 8
FFN_RC = 128


def _shift_rows(u, edge_row, down):
    n = u.shape[0]
    rid = lax.broadcasted_iota(jnp.int32, (V7X_SUBLANES, u.shape[1]), 0)
    if down:
        r = pltpu.roll(u, 1, 0)
        first = jnp.where(rid == 0, edge_row, r[:V7X_SUBLANES])
        return jnp.concatenate([first, r[V7X_SUBLANES:]], axis=0)
    r = pltpu.roll(u, n - 1, 0)
    last = jnp.where(rid == V7X_SUBLANES - 1, edge_row, r[n - V7X_SUBLANES:])
    return jnp.concatenate([r[:n - V7X_SUBLANES], last], axis=0)


def _ffn_kernel(xm_ref, xp_ref, xn_ref, g_ref, sh_ref, sc_ref, gate_ref, win_ref, cw_ref, cb_ref,
                wout_ref, o_ref, h_ref, a_ref, *, tm, nt):
    t = pl.program_id(1)
    gs = g_ref[...] * (1.0 + sc_ref[0])
    sh = sh_ref[0]

    for r in range(0, tm, FFN_RC):
        h_ref[r:r + FFN_RC, :] = _norm_mod(xm_ref[0, r:r + FFN_RC, :], gs, sh).astype(BF16)
    hp = _norm_mod(xp_ref[0], gs, sh) * (t > 0).astype(F32)
    hn = _norm_mod(xn_ref[0], gs, sh) * (t < nt - 1).astype(F32)
    h_ref[tm:tm + 2 * FFN_HALO, :] = jnp.concatenate([hp, hn], axis=0).astype(BF16)

    def conv(u, c0):
        um = u[:tm]
        prev_row = u[tm + FFN_HALO - 1:tm + FFN_HALO]
        next_row = u[tm + FFN_HALO:tm + FFN_HALO + 1]
        w = cw_ref[:, c0:c0 + FFN_CN]
        return (_shift_rows(um, prev_row, True) * w[0:1] + um * w[1:2]
                + _shift_rows(um, next_row, False) * w[2:3] + cb_ref[:, c0:c0 + FFN_CN])

    for j in range(D_FF // FFN_CN):
        c0 = j * FFN_CN
        hs = h_ref[...]
        gate = conv(_dot(hs, win_ref[:, c0:c0 + FFN_CN]), c0)
        val = conv(_dot(hs, win_ref[:, D_FF + c0:D_FF + c0 + FFN_CN]), D_FF + c0)
        a_ref[:, c0:c0 + FFN_CN] = (_silu(gate) * val).astype(BF16)

    o_ref[0] = xm_ref[0] + gate_ref[0] * _dot(a_ref[...], wout_ref[...])


def _ffn_layer(x, g, sh, sc, gate, w_in, conv_w, conv_b, w_out, tm):
    B, L, D = x.shape
    nt = L // tm
    bm = sh.shape[0]
    hb = tm // FFN_HALO
    mod_spec = pl.BlockSpec((1, 1, D), (lambda b, t: (b, 0, 0)) if bm > 1 else (lambda b, t: (0, 0, 0)))
    const = lambda b, t: (0, 0)
    est = (2 * D * 2 * D_FF * 2 + 2 * D_FF * D * 2 + 4 * tm * D * 4
           + (tm + 16) * D * 2 + tm * D_FF * 2 + 6 * (tm + 16) * FFN_CN * 4)
    return pl.pallas_call(
        functools.partial(_ffn_kernel, tm=tm, nt=nt),
        grid=(B, nt),
        in_specs=[
            pl.BlockSpec((1, tm, D), lambda b, t: (b, t, 0)),
            pl.BlockSpec((1, FFN_HALO, D), lambda b, t: (b, jnp.maximum(t * hb - 1, 0), 0)),
            pl.BlockSpec((1, FFN_HALO, D), lambda b, t: (b, jnp.minimum((t + 1) * hb, L // FFN_HALO - 1), 0)),
            pl.BlockSpec((1, D), const),
            mod_spec, mod_spec, mod_spec,
            pl.BlockSpec((D, 2 * D_FF), const),
            pl.BlockSpec((3, 2 * D_FF), const),
            pl.BlockSpec((1, 2 * D_FF), const),
            pl.BlockSpec((D_FF, D), const),
        ],
        out_specs=pl.BlockSpec((1, tm, D), lambda b, t: (b, t, 0)),
        out_shape=jax.ShapeDtypeStruct((B, L, D), F32),
        scratch_shapes=[
            pltpu.VMEM((tm + 2 * FFN_HALO, D), BF16),
            pltpu.VMEM((tm, D_FF), BF16),
        ],
        compiler_params=pltpu.CompilerParams(
            dimension_semantics=("arbitrary", "arbitrary"),
            vmem_limit_bytes=_vmem_limit(est),
        ),
        name="conv_ffn",
    )(x, x, x, g.reshape(1, D), sh, sc, gate, w_in, conv_w, conv_b.reshape(1, 2 * D_FF), w_out)


def _mod_spec(bm):
    return pl.BlockSpec((1, 1, D_MODEL), (lambda b, t: (b, 0, 0)) if bm > 1 else (lambda b, t: (0, 0, 0)))


_CONST2 = lambda b, t: (0, 0)


def _proj_res_kernel(x_ref, a_ref, w_ref, gate_ref, o_ref):
    o_ref[0] = x_ref[0] + gate_ref[0] * _dot(a_ref[0], w_ref[...])


def _proj_res(x, a, w, gate, tm):
    B, L, D = x.shape
    kdim = a.shape[-1]
    est = 2 * kdim * D * 2 + 4 * tm * D * 4 + 2 * tm * kdim * 2 + tm * D * 4
    return pl.pallas_call(
        _proj_res_kernel,
        grid=(B, L // tm),
        in_specs=[
            pl.BlockSpec((1, tm, D), lambda b, t: (b, t, 0)),
            pl.BlockSpec((1, tm, kdim), lambda b, t: (b, t, 0)),
            pl.BlockSpec((kdim, D), _CONST2),
            _mod_spec(gate.shape[0]),
        ],
        out_specs=pl.BlockSpec((1, tm, D), lambda b, t: (b, t, 0)),
        out_shape=jax.ShapeDtypeStruct((B, L, D), F32),
        compiler_params=pltpu.CompilerParams(
            dimension_semantics=("arbitrary", "arbitrary"), vmem_limit_bytes=_vmem_limit(est)),
        name="proj_res",
    )(x, a, w, gate)


def _head_rms(y, e, et, width):
    s_hi, s_lo = _split_bf16(y * y)
    ss = _dot(s_hi, e[:width]) + _dot(s_lo, e[:width])
    r_hi, r_lo = _split_bf16(lax.rsqrt(ss * (1.0 / HEAD_DIM) + EPS))
    return _dot(r_hi, et[:, :width]) + _dot(r_lo, et[:, :width])


def _rope(y, cos, sin_signed, first):
    outs = []
    for c0 in range(0, y.shape[1], V7X_LANES):
        ys = y[:, c0:c0 + V7X_LANES]
        partner = jnp.where(first, pltpu.roll(ys, V7X_LANES - 16, 1), pltpu.roll(ys, 16, 1))
        outs.append(ys * cos + partner * sin_signed)
    return jnp.concatenate(outs, axis=1)


def _qkv_kernel(*refs, rope, keep_f32):
    x_ref, g_ref, sh_ref, sc_ref, w_ref, qn_ref, kn_ref, e_ref, et_ref = refs[:9]
    refs = refs[9:]
    if rope:
        cos_ref, sin_ref, cost_ref, sint_ref = refs[:4]
        refs = refs[4:]
    q_ref, k_ref, v_ref = refs[:3]
    gs = g_ref[...] * (1.0 + sc_ref[0])
    h = _norm_mod(x_ref[0], gs, sh_ref[0]).astype(BF16)
    qkv = _dot(h, w_ref[...])
    k = qkv[:, D_MODEL:D_MODEL + KV_DIM]
    v = qkv[:, D_MODEL + KV_DIM:]
    k = k * _head_rms(k, e_ref[...], et_ref[...], KV_DIM) * kn_ref[...]
    if keep_f32:
        refs[3][0] = k
        refs[4][0] = v
    if rope:
        lane = lax.broadcasted_iota(jnp.int32, (x_ref.shape[1], V7X_LANES), 1)
        k = _rope(k, cos_ref[...], sin_ref[...], (lane % 32) < 16)
    k_ref[0] = k.astype(BF16)
    _put_blocks(v_ref, 0, v.T.astype(BF16))
    qt = qkv[:, :D_MODEL].T
    quarter = HEAD_DIM // 4
    for hd in range(N_HEADS):
        qh = qt[hd * HEAD_DIM:(hd + 1) * HEAD_DIM]
        ms = jnp.sum(qh * qh, axis=0, keepdims=True) * (1.0 / HEAD_DIM)
        qh = qh * lax.rsqrt(ms + EPS) * qn_ref[...]
        if rope:
            parts = [qh[i * quarter:(i + 1) * quarter] for i in range(4)]
            partner = jnp.concatenate([parts[1], parts[0], parts[3], parts[2]], axis=0)
            qh = qh * cost_ref[...] + partner * sint_ref[...]
        _put_blocks(q_ref, hd * HEAD_DIM, qh.astype(BF16))


def _put_blocks(ref, row0, val):
    for i in range(ref.shape[1]):
        ref[0, i, row0:row0 + val.shape[0], :] = val[:, i * BLOCK:(i + 1) * BLOCK]


def _head_selectors():
    col = np.arange(D_MODEL)[:, None] // HEAD_DIM
    e = (col == np.arange(V7X_LANES)[None, :]).astype(np.float32)
    return jnp.asarray(e, BF16), jnp.asarray(e.T, BF16)


def _rope_tables(L):
    half = HEAD_DIM // 2
    rows = jnp.repeat(jnp.arange(L // GRID_W), GRID_W)
    cols = jnp.tile(jnp.arange(GRID_W), L // GRID_W)
    inv_freq = 1.0 / (ROPE_THETA ** (jnp.arange(0, half, 2, dtype=F32) / half))
    ar = rows.astype(F32)[:, None] * inv_freq
    ac = cols.astype(F32)[:, None] * inv_freq
    cos = jnp.concatenate([jnp.cos(ar), jnp.cos(ar), jnp.cos(ac), jnp.cos(ac)], axis=1)
    sin = jnp.concatenate([-jnp.sin(ar), jnp.sin(ar), -jnp.sin(ac), jnp.sin(ac)], axis=1)
    return jnp.tile(cos, (1, 2)), jnp.tile(sin, (1, 2)), cos.T, sin.T


def _qkv_layer(x, g, sh, sc, w_qkv, q_norm, k_norm, tm, rope):
    B, L, D = x.shape
    keep_f32 = not rope
    e, et = _head_selectors()
    qn = jnp.broadcast_to((q_norm * (HEAD_DIM ** -0.5 * LOG2E))[:, None], (HEAD_DIM, tm))
    kn = jnp.tile(k_norm, N_KV_HEADS).reshape(1, KV_DIM)
    ms = _mod_spec(sh.shape[0])
    in_specs = [
        pl.BlockSpec((1, tm, D), lambda b, t: (b, t, 0)),
        pl.BlockSpec((1, D), _CONST2), ms, ms,
        pl.BlockSpec((D, QKV_DIM), _CONST2),
        pl.BlockSpec((HEAD_DIM, tm), _CONST2),
        pl.BlockSpec((1, KV_DIM), _CONST2),
        pl.BlockSpec((KV_DIM, V7X_LANES), _CONST2),
        pl.BlockSpec((V7X_LANES, KV_DIM), _CONST2),
    ]
    args = [x, g.reshape(1, D), sh, sc, w_qkv, qn, kn, e[:KV_DIM], et[:, :KV_DIM]]
    if rope:
        cos, sin, cos_t, sin_t = _rope_tables(L)
        in_specs += [pl.BlockSpec((tm, V7X_LANES), lambda b, t: (t, 0))] * 2
        in_specs += [pl.BlockSpec((HEAD_DIM, tm), lambda b, t: (0, t))] * 2
        args += [cos, sin, cos_t, sin_t]
    tok = lambda w: pl.BlockSpec((1, tm, w), lambda b, t: (b, t, 0))
    tok_t = lambda w: pl.BlockSpec((1, tm // BLOCK, w, BLOCK), lambda b, t: (b, t, 0, 0))
    out_specs = [tok_t(D), tok(KV_DIM), tok_t(KV_DIM)]
    out_shape = [jax.ShapeDtypeStruct((B, L // BLOCK, D, BLOCK), BF16), jax.ShapeDtypeStruct((B, L, KV_DIM), BF16),
                 jax.ShapeDtypeStruct((B, L // BLOCK, KV_DIM, BLOCK), BF16)]
    if keep_f32:
        out_specs += [tok(KV_DIM), tok(KV_DIM)]
        out_shape += [jax.ShapeDtypeStruct((B, L, KV_DIM), F32)] * 2
    est = 2 * D * QKV_DIM * 2 + 2 * tm * D * 4 + 10 * tm * QKV_DIM * 4
    return pl.pallas_call(
        functools.partial(_qkv_kernel, rope=rope, keep_f32=keep_f32),
        grid=(B, L // tm),
        in_specs=in_specs,
        out_specs=out_specs,
        out_shape=out_shape,
        compiler_params=pltpu.CompilerParams(
            dimension_semantics=("arbitrary", "arbitrary"), vmem_limit_bytes=_vmem_limit(est)),
        name="qkv_rope" if rope else "qkv",
    )(*args)


ONES_ROWS = V7X_BF16_ROWS


ATTN_QB = 2


def _attend(qt_ref, qb, parts, sink_ref, x_ref, gate_ref, wo_ref, o_ref, ot_ref):
    staged = []
    rows = slice(qb * BLOCK, (qb + 1) * BLOCK)
    for kh in range(N_KV_HEADS):
        heads = [kh * Q_PER_KV + g for g in range(Q_PER_KV)]
        qt = jnp.concatenate([qt_ref[0, qb, h * HEAD_DIM:(h + 1) * HEAD_DIM, :] for h in heads], axis=1)
        sink = jnp.concatenate([jnp.full((1, BLOCK), sink_ref[h] * LOG2E, F32) for h in heads], axis=1)
        scores = []
        m = sink
        for k, _, bias in parts:
            s = _dot(k[:, kh * HEAD_DIM:(kh + 1) * HEAD_DIM], qt)
            if bias is not None:
                s = s + bias
            scores.append(s)
            m = jnp.maximum(m, jnp.max(s, axis=0, keepdims=True))
        staged.append((heads, sink, m, scores))
    for kh, (heads, sink, m, scores) in enumerate(staged):
        acc = jnp.zeros((HEAD_DIM + ONES_ROWS, Q_PER_KV * BLOCK), F32)
        for s, (_, vt, _) in zip(scores, parts):
            p = jnp.exp2(s - m).astype(BF16)
            vt_ext = jnp.concatenate([vt[kh * HEAD_DIM:(kh + 1) * HEAD_DIM],
                                      jnp.ones((ONES_ROWS, vt.shape[1]), BF16)], axis=0)
            acc = acc + _dot(vt_ext, p)
        denom = jnp.exp2(sink - m) + acc[HEAD_DIM:HEAD_DIM + 1]
        out = acc[:HEAD_DIM] / denom
        for g, h in enumerate(heads):
            ot_ref[qb, h * HEAD_DIM:(h + 1) * HEAD_DIM, :] = out[:, g * BLOCK:(g + 1) * BLOCK]
    attn = ot_ref[qb].T.astype(BF16)
    o_ref[0, rows, :] = x_ref[0, rows, :] + gate_ref[0] * _dot(attn, wo_ref[...])


def _attn_latent_kernel(sink_ref, qt_ref, kp_ref, kc_ref, kn_ref, vp_ref, vc_ref, vn_ref, ck_ref, cv_ref, *refs):
    bias_refs, (x_ref, gate_ref, wo_ref, o_ref, ot_ref) = refs[:ATTN_QB], refs[ATTN_QB:]
    ks = [kp_ref[0]] + [kc_ref[0, i * BLOCK:(i + 1) * BLOCK] for i in range(ATTN_QB)] + [kn_ref[0]]
    vs = [vp_ref[0, 0]] + [vc_ref[0, i] for i in range(ATTN_QB)] + [vn_ref[0, 0]]
    for qb in range(ATTN_QB):
        kw = jnp.concatenate(ks[qb:qb + 3], axis=0)
        vw = jnp.concatenate(vs[qb:qb + 3], axis=1)
        _attend(qt_ref, qb, [(kw, vw, bias_refs[qb][0]), (ck_ref[0], cv_ref[0], None)], sink_ref,
                x_ref, gate_ref, wo_ref, o_ref, ot_ref)


def _attn_context_kernel(sink_ref, qt_ref, k_ref, v_ref, x_ref, gate_ref, wo_ref, o_ref, ot_ref):
    vt = jnp.concatenate([v_ref[0, i] for i in range(v_ref.shape[1])], axis=1)
    for qb in range(ATTN_QB):
        _attend(qt_ref, qb, [(k_ref[0], vt, None)], sink_ref, x_ref, gate_ref, wo_ref, o_ref, ot_ref)


def _band_bias():
    sj = np.arange(3 * BLOCK)[:, None]
    qi = np.arange(Q_PER_KV * BLOCK)[None, :] % BLOCK
    band = np.abs(sj - BLOCK - qi) <= WINDOW
    first = band & (sj >= BLOCK)
    last = band & (sj < 2 * BLOCK)
    return jnp.asarray(np.where(np.stack([first, band, last]), 0.0, NEG_INF), F32)


def _attention(x, gate, w_o, qt, k, vt, sink, ctx_k=None, ctx_vt=None):
    B, nqb, D, _ = qt.shape
    L = nqb * BLOCK
    qb = ATTN_QB
    qspec = pl.BlockSpec((1, qb, D, BLOCK), lambda b, n: (b, n, 0, 0))
    xspec = pl.BlockSpec((1, qb * BLOCK, D), lambda b, n: (b, n, 0))
    smem = pl.BlockSpec(memory_space=pltpu.SMEM)
    tail_specs = [xspec, _mod_spec(gate.shape[0]), pl.BlockSpec((D, D), _CONST2)]
    if ctx_k is None:
        kspec = pl.BlockSpec((1, L, KV_DIM), lambda b, n: (b, 0, 0))
        vspec = pl.BlockSpec((1, nqb, KV_DIM, BLOCK), lambda b, n: (b, 0, 0, 0))
        body, in_specs, args = _attn_context_kernel, [smem, qspec, kspec, vspec], (sink, qt, k, vt)
    else:
        P = ctx_k.shape[1]
        prev = lambda n: jnp.maximum(qb * n - 1, 0)
        nxt = lambda n: jnp.minimum(qb * n + qb, nqb - 1)
        which = lambda i: jnp.where(i == 0, 0, jnp.where(i == nqb - 1, 2, 1))
        kspecs = [pl.BlockSpec((1, BLOCK, KV_DIM), lambda b, n: (b, prev(n), 0)),
                  pl.BlockSpec((1, qb * BLOCK, KV_DIM), lambda b, n: (b, n, 0)),
                  pl.BlockSpec((1, BLOCK, KV_DIM), lambda b, n: (b, nxt(n), 0))]
        vspecs = [pl.BlockSpec((1, 1, KV_DIM, BLOCK), lambda b, n: (b, prev(n), 0, 0)),
                  pl.BlockSpec((1, qb, KV_DIM, BLOCK), lambda b, n: (b, n, 0, 0)),
                  pl.BlockSpec((1, 1, KV_DIM, BLOCK), lambda b, n: (b, nxt(n), 0, 0))]
        bias_specs = [pl.BlockSpec((1, 3 * BLOCK, Q_PER_KV * BLOCK), lambda b, n, i=i: (which(qb * n + i), 0, 0))
                      for i in range(qb)]
        body = _attn_latent_kernel
        in_specs = [smem, qspec, *kspecs, *vspecs,
                    pl.BlockSpec((1, P, KV_DIM), lambda b, n: (b, 0, 0)),
                    pl.BlockSpec((1, KV_DIM, P), lambda b, n: (b, 0, 0)), *bias_specs]
        bias = _band_bias()
        args = (sink, qt, k, k, k, vt, vt, vt, ctx_k, ctx_vt, *([bias] * qb))
    return pl.pallas_call(
        body,
        grid=(B, nqb // qb),
        in_specs=in_specs + tail_specs,
        out_specs=xspec,
        out_shape=jax.ShapeDtypeStruct((B, L, D), F32),
        scratch_shapes=[pltpu.VMEM((qb, D, BLOCK), F32)],
        compiler_params=pltpu.CompilerParams(
            dimension_semantics=("arbitrary", "arbitrary"), vmem_limit_bytes=_vmem_limit(36 << 20)),
        name="attn_context" if ctx_k is None else "attn_latent",
    )(*args, x, gate, w_o)


POOL_TM = 256
POOL_HALO = 8


def _pool_kernel(xm_ref, xp_ref, xn_ref, g_ref, sh_ref, sc_ref, gate_ref, band_ref, pw_ref, ps_ref, o_ref,
                 *, nt, seq_len, n_sub):
    t = pl.program_id(1)
    gs = g_ref[...] * (1.0 + sc_ref[0])
    sh = sh_ref[0]
    xm = xm_ref[0]
    hm = _norm_mod(xm, gs, sh)
    hp = _norm_mod(xp_ref[0], gs, sh) * (t > 0).astype(F32)
    hn = _norm_mod(xn_ref[0], gs, sh) * (t < nt - 1).astype(F32)
    gate = gate_ref[0]
    for u in range(n_sub):
        r0 = u * POOL_TM
        rows = slice(r0, r0 + POOL_TM)
        before = hp if u == 0 else hm[r0 - POOL_HALO:r0]
        after = hn if u == n_sub - 1 else hm[r0 + POOL_TM:r0 + POOL_TM + POOL_HALO]
        h_hi, h_lo = _split_bf16(jnp.concatenate([before, hm[rows], after], axis=0))
        tpos = (t * n_sub + u) * POOL_TM + lax.broadcasted_iota(jnp.int32, (POOL_TM, GROUP_DIM), 0)
        for g, win in enumerate(POOL_WINDOWS):
            back = win // 2
            fwd = win - back - 1
            cs = slice(g * GROUP_DIM, (g + 1) * GROUP_DIM)
            band = band_ref[g]
            wsum = _dot(band, h_hi[:, cs]) + _dot(band, h_lo[:, cs])
            cnt = jnp.minimum(tpos + fwd + 1, seq_len) - jnp.maximum(tpos - back, 0)
            pooled = wsum / cnt.astype(F32) - hm[rows, cs]
            mixed = _dot(pooled.astype(BF16), pw_ref[g])
            o_ref[0, rows, cs] = xm[rows, cs] + gate[:, cs] * (mixed * ps_ref[:, cs])


def _pool_bands():
    r = np.arange(POOL_TM)[:, None] + POOL_HALO
    c = np.arange(POOL_TM + 2 * POOL_HALO)[None, :]
    bands = [((c >= r - w // 2) & (c <= r + (w - w // 2 - 1))) for w in POOL_WINDOWS]
    return jnp.asarray(np.stack(bands).astype(np.float32), BF16)


def _pool_layer(x, g, sh, sc, gate, pool_w, pool_scale):
    B, L, D = x.shape
    n_sub = 2 if L % (2 * POOL_TM) == 0 else 1
    tm = n_sub * POOL_TM
    nt = L // tm
    hb = tm // POOL_HALO
    ms = _mod_spec(sh.shape[0])
    ext = POOL_TM + 2 * POOL_HALO
    return pl.pallas_call(
        functools.partial(_pool_kernel, nt=nt, seq_len=L, n_sub=n_sub),
        grid=(B, nt),
        in_specs=[
            pl.BlockSpec((1, tm, D), lambda b, t: (b, t, 0)),
            pl.BlockSpec((1, POOL_HALO, D), lambda b, t: (b, jnp.maximum(t * hb - 1, 0), 0)),
            pl.BlockSpec((1, POOL_HALO, D), lambda b, t: (b, jnp.minimum((t + 1) * hb, L // POOL_HALO - 1), 0)),
            pl.BlockSpec((1, D), _CONST2), ms, ms, ms,
            pl.BlockSpec((N_GROUPS, POOL_TM, ext), lambda b, t: (0, 0, 0)),
            pl.BlockSpec((N_GROUPS, GROUP_DIM, GROUP_DIM), lambda b, t: (0, 0, 0)),
            pl.BlockSpec((1, D), _CONST2),
        ],
        out_specs=pl.BlockSpec((1, tm, D), lambda b, t: (b, t, 0)),
        out_shape=jax.ShapeDtypeStruct((B, L, D), F32),
        compiler_params=pltpu.CompilerParams(
            dimension_semantics=("arbitrary", "arbitrary"), vmem_limit_bytes=_vmem_limit(24 << 20)),
        name="pool_mix",
    )(x, x, x, g.reshape(1, D), sh, sc, gate, _pool_bands(), pool_w, pool_scale.reshape(1, D))


def _dft_tables(n):
    k = np.arange(n)
    ang = 2.0 * np.pi * ((k[:, None] * k[None, :]) % n) / n
    return np.cos(ang), np.sin(ang)


def _bf16_table(a):
    return jnp.asarray(a, F32).astype(BF16)


def _chan_dft_kernel(x_ref, g_ref, sh_ref, sc_ref, cc_ref, sc_tab_ref, y_ref):
    gs = g_ref[...] * (1.0 + sc_ref[0])
    h = _norm_mod(x_ref[0], gs, sh_ref[0]).astype(BF16)
    for g in range(N_GROUPS):
        cs = slice(g * GROUP_DIM, (g + 1) * GROUP_DIM)
        y_ref[0, 0, :, cs] = _dot(h[:, cs], cc_ref[...]).astype(BF16)
        y_ref[0, 1, :, cs] = _dot(h[:, cs], sc_tab_ref[...]).astype(BF16)


def _chan_dft(x, g, sh, sc, tm):
    B, L, D = x.shape
    cc, sn = _dft_tables(GROUP_DIM)
    ms = _mod_spec(sh.shape[0])
    tab = pl.BlockSpec((GROUP_DIM, GROUP_DIM), _CONST2)
    return pl.pallas_call(
        _chan_dft_kernel,
        grid=(B, L // tm),
        in_specs=[pl.BlockSpec((1, tm, D), lambda b, t: (b, t, 0)), pl.BlockSpec((1, D), _CONST2), ms, ms, tab, tab],
        out_specs=pl.BlockSpec((1, 2, tm, D), lambda b, t: (b, 0, t, 0)),
        out_shape=jax.ShapeDtypeStruct((B, 2, L, D), BF16),
        compiler_params=pltpu.CompilerParams(
            dimension_semantics=("arbitrary", "arbitrary"), vmem_limit_bytes=_vmem_limit(12 * tm * D * 4)),
        name="chan_dft",
    )(x, g.reshape(1, D), sh, sc, _bf16_table(cc), _bf16_table(sn))


def _pos_dft_kernel(w_ref, y_ref, o_ref, *, scale):
    o_ref[0] = (_dot(w_ref[...], y_ref[0]) * scale).astype(BF16)


def _pos_dft(y, L, tmo, tn):
    B, _, D = y.shape
    cl, sl = _dft_tables(L)
    w = _bf16_table(np.concatenate([cl, -sl], axis=1))
    scale = 1.0 / math.sqrt(L * GROUP_DIM)
    est = 2 * tmo * 2 * L * 2 + 2 * 2 * L * tn * 2 + 3 * tmo * tn * 4
    return pl.pallas_call(
        functools.partial(_pos_dft_kernel, scale=scale),
        grid=(B, D // tn, L // tmo),
        in_specs=[
            pl.BlockSpec((tmo, 2 * L), lambda b, j, i: (i, 0)),
            pl.BlockSpec((1, 2 * L, tn), lambda b, j, i: (b, 0, j)),
        ],
        out_specs=pl.BlockSpec((1, tmo, tn), lambda b, j, i: (b, i, j)),
        out_shape=jax.ShapeDtypeStruct((B, L, D), BF16),
        compiler_params=pltpu.CompilerParams(
            dimension_semantics=("arbitrary", "arbitrary", "arbitrary"), vmem_limit_bytes=_vmem_limit(est)),
        name="pos_dft",
    )(w, y)


FFT_N = 64
FFT_T = 16
FFT_PITCH = 24


def _fft_stage1_kernel(x_ref, g_ref, sh_ref, sc_ref, cc_ref, ns_ref, tw_ref, y_ref, gbuf_ref):
    gs = g_ref[...] * (1.0 + sc_ref[0])
    h = _norm_mod(x_ref[0].reshape(FFT_N * FFT_T, D_MODEL), gs, sh_ref[0]).astype(BF16)
    for g in range(N_GROUPS):
        cs = slice(g * GROUP_DIM, (g + 1) * GROUP_DIM)
        _put_groups(gbuf_ref.at[0], g * GROUP_DIM, _dot(h[:, cs], cc_ref[...]))
        _put_groups(gbuf_ref.at[1], g * GROUP_DIM, _dot(h[:, cs], ns_ref[...]))
    for t in range(FFT_T):
        sel = pl.ds(t, FFT_N, stride=FFT_PITCH)
        gst = jnp.concatenate([_get_cols(gbuf_ref.at[0], sel), _get_cols(gbuf_ref.at[1], sel)], axis=0)
        y = _dot(tw_ref[t], gst.astype(BF16))
        _put_cols(gbuf_ref.at[0], sel, y[:FFT_N])
        _put_cols(gbuf_ref.at[1], sel, y[FFT_N:])
    for p in range(2):
        y_ref[0, p] = _get_groups(gbuf_ref.at[p]).reshape(FFT_N, FFT_T, D_MODEL).astype(BF16)


def _get_cols(buf_ref, rows):
    return jnp.concatenate([buf_ref[c, rows, :] for c in range(buf_ref.shape[0])], axis=1)


def _put_cols(buf_ref, rows, val):
    for c in range(buf_ref.shape[0]):
        buf_ref[c, rows, :] = val[:, c * V7X_LANES:(c + 1) * V7X_LANES]


def _get_groups(buf_ref):
    val = _get_cols(buf_ref, slice(None))
    return val.reshape(FFT_N, FFT_PITCH, val.shape[1])[:, :FFT_T].reshape(FFT_N * FFT_T, val.shape[1])


def _put_groups(buf_ref, col0, val):
    pad = jnp.zeros((FFT_N, FFT_PITCH - FFT_T, val.shape[1]), val.dtype)
    val = jnp.concatenate([val.reshape(FFT_N, FFT_T, val.shape[1]), pad], axis=1)
    val = val.reshape(FFT_N * FFT_PITCH, val.shape[2])
    for c in range(val.shape[1] // V7X_LANES):
        buf_ref[col0 // V7X_LANES + c] = val[:, c * V7X_LANES:(c + 1) * V7X_LANES]


def _fft_stage2_kernel(y_ref, x_ref, w_ref, gate_ref, r_ref, o_ref, zbuf_ref, *, scale):
    @pl.when((pl.program_id(0) == 0) & (pl.program_id(1) == 0))
    def _():
        zbuf_ref[...] = jnp.zeros(zbuf_ref.shape, F32)

    for f in range(FFT_T):
        yst = jnp.concatenate([y_ref[0, 0, f], y_ref[0, 1, f]], axis=0)
        _put_cols(zbuf_ref, pl.ds(f, FFT_N, stride=FFT_PITCH), _dot(r_ref[...], yst) * scale)
    mix = _dot(_get_groups(zbuf_ref).astype(BF16), w_ref[...])
    x = x_ref[0].reshape(FFT_N * FFT_T, D_MODEL)
    o_ref[0] = (x + gate_ref[0] * mix).reshape(FFT_N, FFT_T, D_MODEL)


def _fnet_fft_layer(x, g, sh, sc, gate, fnet_w):
    B, L, D = x.shape
    n, tt = FFT_N, FFT_T
    x4 = x.reshape(B, n, n, D)
    cc, sn = _dft_tables(GROUP_DIM)
    f2 = np.arange(n)[None, :, None]
    tpos = np.arange(n)[:, None, None] + n * np.arange(n)[None, None, :]
    ang = 2.0 * np.pi * ((f2 * tpos) % L) / L
    c1, s1 = np.cos(ang), np.sin(ang)
    tw = np.concatenate([np.concatenate([c1, s1], axis=2), np.concatenate([-s1, c1], axis=2)], axis=1)
    c2, s2 = _dft_tables(n)
    r = np.concatenate([c2, s2], axis=1)
    ms = _mod_spec(sh.shape[0])
    tab = pl.BlockSpec((GROUP_DIM, GROUP_DIM), _CONST2)
    y = pl.pallas_call(
        _fft_stage1_kernel,
        grid=(B, n // tt),
        in_specs=[
            pl.BlockSpec((1, n, tt, D), lambda b, i: (b, 0, i, 0)),
            pl.BlockSpec((1, D), _CONST2), ms, ms, tab, tab,
            pl.BlockSpec((tt, 2 * n, 2 * n), lambda b, i: (i, 0, 0)),
        ],
        out_specs=pl.BlockSpec((1, 2, n, tt, D), lambda b, i: (b, 0, 0, i, 0)),
        out_shape=jax.ShapeDtypeStruct((B, 2, n, n, D), BF16),
        scratch_shapes=[pltpu.VMEM((2, D // V7X_LANES, n * FFT_PITCH, V7X_LANES), F32)],
        compiler_params=pltpu.CompilerParams(
            dimension_semantics=("arbitrary", "arbitrary"),
            vmem_limit_bytes=_vmem_limit(2 * n * tt * D * (4 + 4 + 4) + 2 * n * tt * D * 4)),
        name="fft_stage1",
    )(x4, g.reshape(1, D), sh, sc, _bf16_table(cc), _bf16_table(-sn), _bf16_table(tw))
    out = pl.pallas_call(
        functools.partial(_fft_stage2_kernel, scale=1.0 / math.sqrt(L * GROUP_DIM)),
        grid=(B, n // tt),
        in_specs=[
            pl.BlockSpec((1, 2, tt, n, D), lambda b, j: (b, 0, j, 0, 0)),
            pl.BlockSpec((1, n, tt, D), lambda b, j: (b, 0, j, 0)),
            pl.BlockSpec((D, D), _CONST2),
            _mod_spec(gate.shape[0]),
            pl.BlockSpec((n, 2 * n), _CONST2),
        ],
        out_specs=pl.BlockSpec((1, n, tt, D), lambda b, j: (b, 0, j, 0)),
        out_shape=jax.ShapeDtypeStruct((B, n, n, D), F32),
        scratch_shapes=[pltpu.VMEM((D // V7X_LANES, n * FFT_PITCH, V7X_LANES), F32)],
        compiler_params=pltpu.CompilerParams(
            dimension_semantics=("arbitrary", "arbitrary"),
            vmem_limit_bytes=_vmem_limit(n * tt * D * (4 + 8 + 8 + 4 + 4) + 2 * D * D * 2)),
        name="fft_stage2",
    )(y, x4, fnet_w, gate, _bf16_table(r))
    return out.reshape(B, L, D)


def _trunk(x, mods, ctx_k, ctx_v, wts, tm):
    (norm_mix, norm_ffn, w_qkv, q_norm, k_norm, sink, w_o, pool_w, pool_scale, fnet_w,
     w_in, conv_w, conv_b, w_out) = wts
    B, L, D = x.shape
    latent = ctx_k is not None
    ks, vs = [], []
    for i in range(DEPTH):
        kind, j = i % N_MIXERS, i // N_MIXERS
        sh1, sc1, g1, sh2, sc2, g2 = mods[i]
        if kind == 0:
            outs = _qkv_layer(x, norm_mix[i], sh1, sc1, w_qkv[j], q_norm[j], k_norm[j], tm, rope=latent)
            q, k, v = outs[:3]
            if latent:
                x = _attention(x, g1, w_o[j], q, k, v, sink[j], ctx_k[j], ctx_v[j])
            else:
                x = _attention(x, g1, w_o[j], q, k, v, sink[j])
                ks.append(outs[3].reshape(B, L, N_KV_HEADS, HEAD_DIM))
                vs.append(outs[4].reshape(B, L, N_KV_HEADS, HEAD_DIM))
        elif kind == 1:
            x = _pool_layer(x, norm_mix[i], sh1, sc1, g1, pool_w[j], pool_scale[j])
        elif L == FFT_N * FFT_N:
            x = _fnet_fft_layer(x, norm_mix[i], sh1, sc1, g1, fnet_w[j])
        else:
            y = _chan_dft(x, norm_mix[i], sh1, sc1, tm).reshape(B, 2 * L, D)
            f = _pos_dft(y, L, min(L, 512), 512 if L > 512 else D)
            x = _proj_res(x, f, fnet_w[j], g1, tm)
        x = _ffn_layer(x, norm_ffn[i], sh2, sc2, g2, w_in[i], conv_w[i], conv_b[i], w_out[i], tm)
    return x, ks, vs


def kernel(x_prompt, x_sample, cache_k, cache_v, c, c_ctx, norm_mix, norm_ffn, ada_w, ada_b, attn_w_qkv,
           attn_q_norm, attn_k_norm, attn_sink, attn_w_o, pool_w, pool_scale, fnet_w, ffn_w_in, ffn_conv_w,
           ffn_conv_b, ffn_w_out):
    nb = c.shape[0]
    cond = jnp.concatenate([c, c_ctx[None], jnp.zeros((ADA_ROWS - nb - 1, D_MODEL), F32)], axis=0)
    mods = _ada_mods(cond, ada_w, ada_b).reshape(DEPTH, ADA_ROWS, 6, 1, D_MODEL)
    mods_sample = [[mods[i, :nb, m] for m in range(6)] for i in range(DEPTH)]
    mods_prompt = [[mods[i, nb:nb + 1, m] for m in range(6)] for i in range(DEPTH)]
    wts = (norm_mix, norm_ffn, attn_w_qkv.astype(BF16), attn_q_norm, attn_k_norm, attn_sink,
           attn_w_o.astype(BF16), pool_w.astype(BF16), pool_scale, fnet_w.astype(BF16),
           ffn_w_in.astype(BF16), ffn_conv_w, ffn_conv_b, ffn_w_out.astype(BF16))
    y_prompt, ks, vs = _trunk(x_prompt, mods_prompt, None, None, wts, tm=256)
    past = cache_k.shape[2]
    ctx_k = [cache_k[:, j].reshape(nb, past, KV_DIM).astype(BF16) for j in range(cache_k.shape[1])]
    ctx_v = [cache_v[:, j].reshape(nb, past, KV_DIM).astype(BF16).transpose(0, 2, 1) for j in range(cache_v.shape[1])]
    y_sample, _, _ = _trunk(x_sample, mods_sample, ctx_k, ctx_v, wts, tm=512)
    return (y_prompt, y_sample, jnp.stack(ks, axis=1), jnp.stack(vs, axis=1))
```

```python
import functools
import math

import jax
import jax.numpy as jnp
import numpy as np
from jax import lax
from jax.experimental import pallas as pl
from jax.experimental.pallas import tpu as pltpu

D_MODEL = 1024
DEPTH = 4
GRID_W = 64
N_MIXERS = 3
N_HEADS = 16
N_KV_HEADS = 4
HEAD_DIM = D_MODEL // N_HEADS
Q_PER_KV = N_HEADS // N_KV_HEADS
KV_DIM = N_KV_HEADS * HEAD_DIM
QKV_DIM = (N_HEADS + 2 * N_KV_HEADS) * HEAD_DIM
WINDOW = 128
BLOCK = 128
ROPE_THETA = 10000.0
POOL_WINDOWS = (2, 4, 8, 16)
N_GROUPS = 4
GROUP_DIM = D_MODEL // N_GROUPS
D_FF = ((8 * D_MODEL // 3 + 127) // 128) * 128
EPS = 1e-6
NEG_INF = -1e30
LOG2E = math.log2(math.e)

V7X_LANES = 128
V7X_SUBLANES = 8
V7X_BF16_ROWS = 16
V7X_MXU_DIM = 256
V7X_VMEM_BYTES = 64 * 1024 * 1024

F32 = jnp.float32
BF16 = jnp.bfloat16


def _vmem_limit(nbytes):
    return int(min(V7X_VMEM_BYTES - (4 << 20), nbytes * 3 // 2 + (4 << 20)))


def _silu(x):
    return x * (1.0 / (1.0 + jnp.exp(-x)))


def _split_bf16(a):
    hi = a.astype(BF16)
    lo = (a - hi.astype(F32)).astype(BF16)
    return hi, lo


def _norm_mod(x, gs, sh):
    ms = jnp.mean(x * x, axis=-1, keepdims=True)
    return x * lax.rsqrt(ms + EPS) * gs + sh


def _dot(a, b):
    return jnp.dot(a, b, preferred_element_type=F32)


ADA_ROWS = 16
ADA_TN = 1536


def _ada_kernel(c_ref, w_ref, b_ref, o_ref):
    s_hi, s_lo = _split_bf16(_silu(c_ref[...]))
    w_hi, w_lo = _split_bf16(w_ref[0])
    acc = _dot(s_hi, w_hi) + (_dot(s_lo, w_hi) + _dot(s_hi, w_lo))
    o_ref[0] = acc + b_ref[0]


def _ada_mods(cond, ada_w, ada_b):
    n_out = 6 * D_MODEL
    blk = D_MODEL * ADA_TN * 4
    return pl.pallas_call(
        _ada_kernel,
        grid=(DEPTH, n_out // ADA_TN),
        in_specs=[
            pl.BlockSpec((ADA_ROWS, D_MODEL), lambda i, j: (0, 0)),
            pl.BlockSpec((1, D_MODEL, ADA_TN), lambda i, j: (i, 0, j)),
            pl.BlockSpec((1, 1, ADA_TN), lambda i, j: (i, 0, j)),
        ],
        out_specs=pl.BlockSpec((1, ADA_ROWS, ADA_TN), lambda i, j: (i, 0, j)),
        out_shape=jax.ShapeDtypeStruct((DEPTH, ADA_ROWS, n_out), F32),
        compiler_params=pltpu.CompilerParams(
            dimension_semantics=("arbitrary", "arbitrary"),
            vmem_limit_bytes=_vmem_limit(2 * blk + 2 * blk),
        ),
        name="ada_mods",
    )(cond, ada_w, ada_b.reshape(DEPTH, 1, n_out))


FFN_CN = 256
FFN_HALO = 8
FFN_RC = 128
FFN_DOT_ROWS = 176


def _row_bounds(m, target):
    n = max(1, round(m / target))
    step = -(-m // (n * V7X_BF16_ROWS)) * V7X_BF16_ROWS
    return list(range(0, m, step)) + [m]


def _shift_rows(u, edge_row, down):
    n = u.shape[0]
    rid = lax.broadcasted_iota(jnp.int32, (V7X_SUBLANES, u.shape[1]), 0)
    if down:
        r = pltpu.roll(u, 1, 0)
        first = jnp.where(rid == 0, edge_row, r[:V7X_SUBLANES])
        return jnp.concatenate([first, r[V7X_SUBLANES:]], axis=0)
    r = pltpu.roll(u, n - 1, 0)
    last = jnp.where(rid == V7X_SUBLANES - 1, edge_row, r[n - V7X_SUBLANES:])
    return jnp.concatenate([r[:n - V7X_SUBLANES], last], axis=0)


def _ffn_kernel(xm_ref, xp_ref, xn_ref, g_ref, sh_ref, sc_ref, gate_ref, win_ref, cw_ref, cb_ref,
                wout_ref, o_ref, h_ref, a_ref, *, tm, nt):
    t = pl.program_id(1)
    gs = g_ref[...] * (1.0 + sc_ref[0])
    sh = sh_ref[0]

    for r in range(0, tm, FFN_RC):
        h_ref[r:r + FFN_RC, :] = _norm_mod(xm_ref[0, r:r + FFN_RC, :], gs, sh).astype(BF16)
    hp = _norm_mod(xp_ref[0], gs, sh) * (t > 0).astype(F32)
    hn = _norm_mod(xn_ref[0], gs, sh) * (t < nt - 1).astype(F32)
    h_ref[tm:tm + 2 * FFN_HALO, :] = jnp.concatenate([hp, hn], axis=0).astype(BF16)

    def conv(u, c0):
        um = u[:tm]
        prev_row = u[tm + FFN_HALO - 1:tm + FFN_HALO]
        next_row = u[tm + FFN_HALO:tm + FFN_HALO + 1]
        w = cw_ref[:, c0:c0 + FFN_CN]
        return (_shift_rows(um, prev_row, True) * w[0:1] + um * w[1:2]
                + _shift_rows(um, next_row, False) * w[2:3] + cb_ref[:, c0:c0 + FFN_CN])

    bounds = _row_bounds(tm + 2 * FFN_HALO, FFN_DOT_ROWS)

    def up_proj(w):
        return jnp.concatenate([_dot(h_ref[a:b, :], w) for a, b in zip(bounds[:-1], bounds[1:])], axis=0)

    for j in range(D_FF // FFN_CN):
        c0 = j * FFN_CN
        gate = conv(up_proj(win_ref[:, c0:c0 + FFN_CN]), c0)
        val = conv(up_proj(win_ref[:, D_FF + c0:D_FF + c0 + FFN_CN]), D_FF + c0)
        a_ref[:, c0:c0 + FFN_CN] = (_silu(gate) * val).astype(BF16)

    o_ref[0] = xm_ref[0] + gate_ref[0] * _dot(a_ref[...], wout_ref[...])


def _ffn_layer(x, g, sh, sc, gate, w_in, conv_w, conv_b, w_out, tm):
    B, L, D = x.shape
    nt = L // tm
    bm = sh.shape[0]
    hb = tm // FFN_HALO
    mod_spec = pl.BlockSpec((1, 1, D), (lambda b, t: (b, 0, 0)) if bm > 1 else (lambda b, t: (0, 0, 0)))
    const = lambda b, t: (0, 0)
    est = (2 * D * 2 * D_FF * 2 + 2 * D_FF * D * 2 + 4 * tm * D * 4
           + (tm + 16) * D * 2 + tm * D_FF * 2 + 6 * (tm + 16) * FFN_CN * 4)
    return pl.pallas_call(
        functools.partial(_ffn_kernel, tm=tm, nt=nt),
        grid=(B, nt),
        in_specs=[
            pl.BlockSpec((1, tm, D), lambda b, t: (b, t, 0)),
            pl.BlockSpec((1, FFN_HALO, D), lambda b, t: (b, jnp.maximum(t * hb - 1, 0), 0)),
            pl.BlockSpec((1, FFN_HALO, D), lambda b, t: (b, jnp.minimum((t + 1) * hb, L // FFN_HALO - 1), 0)),
            pl.BlockSpec((1, D), const),
            mod_spec, mod_spec, mod_spec,
            pl.BlockSpec((D, 2 * D_FF), const),
            pl.BlockSpec((3, 2 * D_FF), const),
            pl.BlockSpec((1, 2 * D_FF), const),
            pl.BlockSpec((D_FF, D), const),
        ],
        out_specs=pl.BlockSpec((1, tm, D), lambda b, t: (b, t, 0)),
        out_shape=jax.ShapeDtypeStruct((B, L, D), F32),
        scratch_shapes=[
            pltpu.VMEM((tm + 2 * FFN_HALO, D), BF16),
            pltpu.VMEM((tm, D_FF), BF16),
        ],
        compiler_params=pltpu.CompilerParams(
            dimension_semantics=("arbitrary", "arbitrary"),
            vmem_limit_bytes=_vmem_limit(est),
        ),
        name="conv_ffn",
    )(x, x, x, g.reshape(1, D), sh, sc, gate, w_in, conv_w, conv_b.reshape(1, 2 * D_FF), w_out)


def _mod_spec(bm):
    return pl.BlockSpec((1, 1, D_MODEL), (lambda b, t: (b, 0, 0)) if bm > 1 else (lambda b, t: (0, 0, 0)))


_CONST2 = lambda b, t: (0, 0)


def _proj_res_kernel(x_ref, a_ref, w_ref, gate_ref, o_ref):
    o_ref[0] = x_ref[0] + gate_ref[0] * _dot(a_ref[0], w_ref[...])


def _proj_res(x, a, w, gate, tm):
    B, L, D = x.shape
    kdim = a.shape[-1]
    est = 2 * kdim * D * 2 + 4 * tm * D * 4 + 2 * tm * kdim * 2 + tm * D * 4
    return pl.pallas_call(
        _proj_res_kernel,
        grid=(B, L // tm),
        in_specs=[
            pl.BlockSpec((1, tm, D), lambda b, t: (b, t, 0)),
            pl.BlockSpec((1, tm, kdim), lambda b, t: (b, t, 0)),
            pl.BlockSpec((kdim, D), _CONST2),
            _mod_spec(gate.shape[0]),
        ],
        out_specs=pl.BlockSpec((1, tm, D), lambda b, t: (b, t, 0)),
        out_shape=jax.ShapeDtypeStruct((B, L, D), F32),
        compiler_params=pltpu.CompilerParams(
            dimension_semantics=("arbitrary", "arbitrary"), vmem_limit_bytes=_vmem_limit(est)),
        name="proj_res",
    )(x, a, w, gate)


def _head_rms(y, e, et, width):
    s_hi, s_lo = _split_bf16(y * y)
    ss = _dot(s_hi, e[:width]) + _dot(s_lo, e[:width])
    r_hi, r_lo = _split_bf16(lax.rsqrt(ss * (1.0 / HEAD_DIM) + EPS))
    return _dot(r_hi, et[:, :width]) + _dot(r_lo, et[:, :width])


def _rope(y, cos, sin_signed, first):
    outs = []
    for c0 in range(0, y.shape[1], V7X_LANES):
        ys = y[:, c0:c0 + V7X_LANES]
        partner = jnp.where(first, pltpu.roll(ys, V7X_LANES - 16, 1), pltpu.roll(ys, 16, 1))
        outs.append(ys * cos + partner * sin_signed)
    return jnp.concatenate(outs, axis=1)


def _qkv_kernel(*refs, rope, keep_f32):
    x_ref, g_ref, sh_ref, sc_ref, w_ref, qn_ref, kn_ref, e_ref, et_ref = refs[:9]
    refs = refs[9:]
    if rope:
        cos_ref, sin_ref, cost_ref, sint_ref = refs[:4]
        refs = refs[4:]
    q_ref, k_ref, v_ref = refs[:3]
    gs = g_ref[...] * (1.0 + sc_ref[0])
    h = _norm_mod(x_ref[0], gs, sh_ref[0]).astype(BF16)
    qkv = _dot(h, w_ref[...])
    k = qkv[:, D_MODEL:D_MODEL + KV_DIM]
    v = qkv[:, D_MODEL + KV_DIM:]
    k = k * _head_rms(k, e_ref[...], et_ref[...], KV_DIM) * kn_ref[...]
    if keep_f32:
        refs[3][0] = k
        refs[4][0] = v
    if rope:
        lane = lax.broadcasted_iota(jnp.int32, (x_ref.shape[1], V7X_LANES), 1)
        k = _rope(k, cos_ref[...], sin_ref[...], (lane % 32) < 16)
    k_ref[0] = k.astype(BF16)
    _put_blocks(v_ref, 0, v.T.astype(BF16))
    qt = qkv[:, :D_MODEL].T
    quarter = HEAD_DIM // 4
    for hd in range(N_HEADS):
        qh = qt[hd * HEAD_DIM:(hd + 1) * HEAD_DIM]
        ms = jnp.sum(qh * qh, axis=0, keepdims=True) * (1.0 / HEAD_DIM)
        qh = qh * lax.rsqrt(ms + EPS) * qn_ref[...]
        if rope:
            parts = [qh[i * quarter:(i + 1) * quarter] for i in range(4)]
            partner = jnp.concatenate([parts[1], parts[0], parts[3], parts[2]], axis=0)
            qh = qh * cost_ref[...] + partner * sint_ref[...]
        _put_blocks(q_ref, hd * HEAD_DIM, qh.astype(BF16))


def _put_blocks(ref, row0, val):
    for i in range(ref.shape[1]):
        ref[0, i, row0:row0 + val.shape[0], :] = val[:, i * BLOCK:(i + 1) * BLOCK]


def _head_selectors():
    col = np.arange(D_MODEL)[:, None] // HEAD_DIM
    e = (col == np.arange(V7X_LANES)[None, :]).astype(np.float32)
    return jnp.asarray(e, BF16), jnp.asarray(e.T, BF16)


def _rope_tables(L):
    half = HEAD_DIM // 2
    rows = jnp.repeat(jnp.arange(L // GRID_W), GRID_W)
    cols = jnp.tile(jnp.arange(GRID_W), L // GRID_W)
    inv_freq = 1.0 / (ROPE_THETA ** (jnp.arange(0, half, 2, dtype=F32) / half))
    ar = rows.astype(F32)[:, None] * inv_freq
    ac = cols.astype(F32)[:, None] * inv_freq
    cos = jnp.concatenate([jnp.cos(ar), jnp.cos(ar), jnp.cos(ac), jnp.cos(ac)], axis=1)
    sin = jnp.concatenate([-jnp.sin(ar), jnp.sin(ar), -jnp.sin(ac), jnp.sin(ac)], axis=1)
    return jnp.tile(cos, (1, 2)), jnp.tile(sin, (1, 2)), cos.T, sin.T


def _qkv_layer(x, g, sh, sc, w_qkv, q_norm, k_norm, tm, rope):
    B, L, D = x.shape
    keep_f32 = not rope
    e, et = _head_selectors()
    qn = jnp.broadcast_to((q_norm * (HEAD_DIM ** -0.5 * LOG2E))[:, None], (HEAD_DIM, tm))
    kn = jnp.tile(k_norm, N_KV_HEADS).reshape(1, KV_DIM)
    ms = _mod_spec(sh.shape[0])
    in_specs = [
        pl.BlockSpec((1, tm, D), lambda b, t: (b, t, 0)),
        pl.BlockSpec((1, D), _CONST2), ms, ms,
        pl.BlockSpec((D, QKV_DIM), _CONST2),
        pl.BlockSpec((HEAD_DIM, tm), _CONST2),
        pl.BlockSpec((1, KV_DIM), _CONST2),
        pl.BlockSpec((KV_DIM, V7X_LANES), _CONST2),
        pl.BlockSpec((V7X_LANES, KV_DIM), _CONST2),
    ]
    args = [x, g.reshape(1, D), sh, sc, w_qkv, qn, kn, e[:KV_DIM], et[:, :KV_DIM]]
    if rope:
        cos, sin, cos_t, sin_t = _rope_tables(L)
        in_specs += [pl.BlockSpec((tm, V7X_LANES), lambda b, t: (t, 0))] * 2
        in_specs += [pl.BlockSpec((HEAD_DIM, tm), lambda b, t: (0, t))] * 2
        args += [cos, sin, cos_t, sin_t]
    tok = lambda w: pl.BlockSpec((1, tm, w), lambda b, t: (b, t, 0))
    tok_t = lambda w: pl.BlockSpec((1, tm // BLOCK, w, BLOCK), lambda b, t: (b, t, 0, 0))
    out_specs = [tok_t(D), tok(KV_DIM), tok_t(KV_DIM)]
    out_shape = [jax.ShapeDtypeStruct((B, L // BLOCK, D, BLOCK), BF16), jax.ShapeDtypeStruct((B, L, KV_DIM), BF16),
                 jax.ShapeDtypeStruct((B, L // BLOCK, KV_DIM, BLOCK), BF16)]
    if keep_f32:
        out_specs += [tok(KV_DIM), tok(KV_DIM)]
        out_shape += [jax.ShapeDtypeStruct((B, L, KV_DIM), F32)] * 2
    est = 2 * D * QKV_DIM * 2 + 2 * tm * D * 4 + 10 * tm * QKV_DIM * 4
    return pl.pallas_call(
        functools.partial(_qkv_kernel, rope=rope, keep_f32=keep_f32),
        grid=(B, L // tm),
        in_specs=in_specs,
        out_specs=out_specs,
        out_shape=out_shape,
        compiler_params=pltpu.CompilerParams(
            dimension_semantics=("arbitrary", "arbitrary"), vmem_limit_bytes=_vmem_limit(est)),
        name="qkv_rope" if rope else "qkv",
    )(*args)


ONES_ROWS = V7X_BF16_ROWS


ATTN_QB = 2


def _attend(qt_ref, qb, parts, sink_ref, x_ref, gate_ref, wo_ref, o_ref, ot_ref):
    staged = []
    rows = slice(qb * BLOCK, (qb + 1) * BLOCK)
    for kh in range(N_KV_HEADS):
        heads = [kh * Q_PER_KV + g for g in range(Q_PER_KV)]
        qt = jnp.concatenate([qt_ref[0, qb, h * HEAD_DIM:(h + 1) * HEAD_DIM, :] for h in heads], axis=1)
        sink = jnp.concatenate([jnp.full((1, BLOCK), sink_ref[h] * LOG2E, F32) for h in heads], axis=1)
        scores = []
        m = sink
        for k, _, bias in parts:
            s = _dot(k[:, kh * HEAD_DIM:(kh + 1) * HEAD_DIM], qt)
            if bias is not None:
                s = s + bias
            scores.append(s)
            m = jnp.maximum(m, jnp.max(s, axis=0, keepdims=True))
        staged.append((heads, sink, m, scores))
    for kh, (heads, sink, m, scores) in enumerate(staged):
        acc = jnp.zeros((HEAD_DIM + ONES_ROWS, Q_PER_KV * BLOCK), F32)
        for s, (_, vt, _) in zip(scores, parts):
            p = jnp.exp2(s - m).astype(BF16)
            vt_ext = jnp.concatenate([vt[kh * HEAD_DIM:(kh + 1) * HEAD_DIM],
                                      jnp.ones((ONES_ROWS, vt.shape[1]), BF16)], axis=0)
            acc = acc + _dot(vt_ext, p)
        denom = jnp.exp2(sink - m) + acc[HEAD_DIM:HEAD_DIM + 1]
        out = acc[:HEAD_DIM] / denom
        for g, h in enumerate(heads):
            ot_ref[qb, h * HEAD_DIM:(h + 1) * HEAD_DIM, :] = out[:, g * BLOCK:(g + 1) * BLOCK]
    attn = ot_ref[qb].T.astype(BF16)
    o_ref[0, rows, :] = x_ref[0, rows, :] + gate_ref[0] * _dot(attn, wo_ref[...])


def _attn_latent_kernel(sink_ref, qt_ref, kp_ref, kc_ref, kn_ref, vp_ref, vc_ref, vn_ref, ck_ref, cv_ref, *refs):
    bias_refs, (x_ref, gate_ref, wo_ref, o_ref, ot_ref) = refs[:ATTN_QB], refs[ATTN_QB:]
    ks = [kp_ref[0]] + [kc_ref[0, i * BLOCK:(i + 1) * BLOCK] for i in range(ATTN_QB)] + [kn_ref[0]]
    vs = [vp_ref[0, 0]] + [vc_ref[0, i] for i in range(ATTN_QB)] + [vn_ref[0, 0]]
    for qb in range(ATTN_QB):
        kw = jnp.concatenate(ks[qb:qb + 3], axis=0)
        vw = jnp.concatenate(vs[qb:qb + 3], axis=1)
        _attend(qt_ref, qb, [(kw, vw, bias_refs[qb][0]), (ck_ref[0], cv_ref[0], None)], sink_ref,
                x_ref, gate_ref, wo_ref, o_ref, ot_ref)


def _attn_context_kernel(sink_ref, qt_ref, k_ref, v_ref, x_ref, gate_ref, wo_ref, o_ref, ot_ref):
    vt = jnp.concatenate([v_ref[0, i] for i in range(v_ref.shape[1])], axis=1)
    for qb in range(ATTN_QB):
        _attend(qt_ref, qb, [(k_ref[0], vt, None)], sink_ref, x_ref, gate_ref, wo_ref, o_ref, ot_ref)


def _band_bias():
    sj = np.arange(3 * BLOCK)[:, None]
    qi = np.arange(Q_PER_KV * BLOCK)[None, :] % BLOCK
    band = np.abs(sj - BLOCK - qi) <= WINDOW
    first = band & (sj >= BLOCK)
    last = band & (sj < 2 * BLOCK)
    return jnp.asarray(np.where(np.stack([first, band, last]), 0.0, NEG_INF), F32)


def _attention(x, gate, w_o, qt, k, vt, sink, ctx_k=None, ctx_vt=None):
    B, nqb, D, _ = qt.shape
    L = nqb * BLOCK
    qb = ATTN_QB
    qspec = pl.BlockSpec((1, qb, D, BLOCK), lambda b, n: (b, n, 0, 0))
    xspec = pl.BlockSpec((1, qb * BLOCK, D), lambda b, n: (b, n, 0))
    smem = pl.BlockSpec(memory_space=pltpu.SMEM)
    tail_specs = [xspec, _mod_spec(gate.shape[0]), pl.BlockSpec((D, D), _CONST2)]
    if ctx_k is None:
        kspec = pl.BlockSpec((1, L, KV_DIM), lambda b, n: (b, 0, 0))
        vspec = pl.BlockSpec((1, nqb, KV_DIM, BLOCK), lambda b, n: (b, 0, 0, 0))
        body, in_specs, args = _attn_context_kernel, [smem, qspec, kspec, vspec], (sink, qt, k, vt)
    else:
        P = ctx_k.shape[1]
        prev = lambda n: jnp.maximum(qb * n - 1, 0)
        nxt = lambda n: jnp.minimum(qb * n + qb, nqb - 1)
        which = lambda i: jnp.where(i == 0, 0, jnp.where(i == nqb - 1, 2, 1))
        kspecs = [pl.BlockSpec((1, BLOCK, KV_DIM), lambda b, n: (b, prev(n), 0)),
                  pl.BlockSpec((1, qb * BLOCK, KV_DIM), lambda b, n: (b, n, 0)),
                  pl.BlockSpec((1, BLOCK, KV_DIM), lambda b, n: (b, nxt(n), 0))]
        vspecs = [pl.BlockSpec((1, 1, KV_DIM, BLOCK), lambda b, n: (b, prev(n), 0, 0)),
                  pl.BlockSpec((1, qb, KV_DIM, BLOCK), lambda b, n: (b, n, 0, 0)),
                  pl.BlockSpec((1, 1, KV_DIM, BLOCK), lambda b, n: (b, nxt(n), 0, 0))]
        bias_specs = [pl.BlockSpec((1, 3 * BLOCK, Q_PER_KV * BLOCK), lambda b, n, i=i: (which(qb * n + i), 0, 0))
                      for i in range(qb)]
        body = _attn_latent_kernel
        in_specs = [smem, qspec, *kspecs, *vspecs,
                    pl.BlockSpec((1, P, KV_DIM), lambda b, n: (b, 0, 0)),
                    pl.BlockSpec((1, KV_DIM, P), lambda b, n: (b, 0, 0)), *bias_specs]
        bias = _band_bias()
        args = (sink, qt, k, k, k, vt, vt, vt, ctx_k, ctx_vt, *([bias] * qb))
    return pl.pallas_call(
        body,
        grid=(B, nqb // qb),
        in_specs=in_specs + tail_specs,
        out_specs=xspec,
        out_shape=jax.ShapeDtypeStruct((B, L, D), F32),
        scratch_shapes=[pltpu.VMEM((qb, D, BLOCK), F32)],
        compiler_params=pltpu.CompilerParams(
            dimension_semantics=("arbitrary", "arbitrary"), vmem_limit_bytes=_vmem_limit(36 << 20)),
        name="attn_context" if ctx_k is None else "attn_latent",
    )(*args, x, gate, w_o)


POOL_TM = 256
POOL_HALO = 8


def _pool_kernel(xm_ref, xp_ref, xn_ref, g_ref, sh_ref, sc_ref, gate_ref, band_ref, pw_ref, ps_ref, o_ref,
                 *, nt, seq_len, n_sub):
    t = pl.program_id(1)
    gs = g_ref[...] * (1.0 + sc_ref[0])
    sh = sh_ref[0]
    xm = xm_ref[0]
    hm = _norm_mod(xm, gs, sh)
    hp = _norm_mod(xp_ref[0], gs, sh) * (t > 0).astype(F32)
    hn = _norm_mod(xn_ref[0], gs, sh) * (t < nt - 1).astype(F32)
    gate = gate_ref[0]
    for u in range(n_sub):
        r0 = u * POOL_TM
        rows = slice(r0, r0 + POOL_TM)
        before = hp if u == 0 else hm[r0 - POOL_HALO:r0]
        after = hn if u == n_sub - 1 else hm[r0 + POOL_TM:r0 + POOL_TM + POOL_HALO]
        h_hi, h_lo = _split_bf16(jnp.concatenate([before, hm[rows], after], axis=0))
        tpos = (t * n_sub + u) * POOL_TM + lax.broadcasted_iota(jnp.int32, (POOL_TM, GROUP_DIM), 0)
        for g, win in enumerate(POOL_WINDOWS):
            back = win // 2
            fwd = win - back - 1
            cs = slice(g * GROUP_DIM, (g + 1) * GROUP_DIM)
            band = band_ref[g]
            wsum = _dot(band, h_hi[:, cs]) + _dot(band, h_lo[:, cs])
            cnt = jnp.minimum(tpos + fwd + 1, seq_len) - jnp.maximum(tpos - back, 0)
            pooled = wsum / cnt.astype(F32) - hm[rows, cs]
            mixed = _dot(pooled.astype(BF16), pw_ref[g])
            o_ref[0, rows, cs] = xm[rows, cs] + gate[:, cs] * (mixed * ps_ref[:, cs])


def _pool_bands():
    r = np.arange(POOL_TM)[:, None] + POOL_HALO
    c = np.arange(POOL_TM + 2 * POOL_HALO)[None, :]
    bands = [((c >= r - w // 2) & (c <= r + (w - w // 2 - 1))) for w in POOL_WINDOWS]
    return jnp.asarray(np.stack(bands).astype(np.float32), BF16)


def _pool_layer(x, g, sh, sc, gate, pool_w, pool_scale):
    B, L, D = x.shape
    n_sub = 2 if L % (2 * POOL_TM) == 0 else 1
    tm = n_sub * POOL_TM
    nt = L // tm
    hb = tm // POOL_HALO
    ms = _mod_spec(sh.shape[0])
    ext = POOL_TM + 2 * POOL_HALO
    return pl.pallas_call(
        functools.partial(_pool_kernel, nt=nt, seq_len=L, n_sub=n_sub),
        grid=(B, nt),
        in_specs=[
            pl.BlockSpec((1, tm, D), lambda b, t: (b, t, 0)),
            pl.BlockSpec((1, POOL_HALO, D), lambda b, t: (b, jnp.maximum(t * hb - 1, 0), 0)),
            pl.BlockSpec((1, POOL_HALO, D), lambda b, t: (b, jnp.minimum((t + 1) * hb, L // POOL_HALO - 1), 0)),
            pl.BlockSpec((1, D), _CONST2), ms, ms, ms,
            pl.BlockSpec((N_GROUPS, POOL_TM, ext), lambda b, t: (0, 0, 0)),
            pl.BlockSpec((N_GROUPS, GROUP_DIM, GROUP_DIM), lambda b, t: (0, 0, 0)),
            pl.BlockSpec((1, D), _CONST2),
        ],
        out_specs=pl.BlockSpec((1, tm, D), lambda b, t: (b, t, 0)),
        out_shape=jax.ShapeDtypeStruct((B, L, D), F32),
        compiler_params=pltpu.CompilerParams(
            dimension_semantics=("arbitrary", "arbitrary"), vmem_limit_bytes=_vmem_limit(24 << 20)),
        name="pool_mix",
    )(x, x, x, g.reshape(1, D), sh, sc, gate, _pool_bands(), pool_w, pool_scale.reshape(1, D))


def _dft_tables(n):
    k = np.arange(n)
    ang = 2.0 * np.pi * ((k[:, None] * k[None, :]) % n) / n
    return np.cos(ang), np.sin(ang)


def _bf16_table(a):
    return jnp.asarray(a, F32).astype(BF16)


def _chan_dft_kernel(x_ref, g_ref, sh_ref, sc_ref, cc_ref, sc_tab_ref, y_ref):
    gs = g_ref[...] * (1.0 + sc_ref[0])
    h = _norm_mod(x_ref[0], gs, sh_ref[0]).astype(BF16)
    for g in range(N_GROUPS):
        cs = slice(g * GROUP_DIM, (g + 1) * GROUP_DIM)
        y_ref[0, 0, :, cs] = _dot(h[:, cs], cc_ref[...]).astype(BF16)
        y_ref[0, 1, :, cs] = _dot(h[:, cs], sc_tab_ref[...]).astype(BF16)


def _chan_dft(x, g, sh, sc, tm):
    B, L, D = x.shape
    cc, sn = _dft_tables(GROUP_DIM)
    ms = _mod_spec(sh.shape[0])
    tab = pl.BlockSpec((GROUP_DIM, GROUP_DIM), _CONST2)
    return pl.pallas_call(
        _chan_dft_kernel,
        grid=(B, L // tm),
        in_specs=[pl.BlockSpec((1, tm, D), lambda b, t: (b, t, 0)), pl.BlockSpec((1, D), _CONST2), ms, ms, tab, tab],
        out_specs=pl.BlockSpec((1, 2, tm, D), lambda b, t: (b, 0, t, 0)),
        out_shape=jax.ShapeDtypeStruct((B, 2, L, D), BF16),
        compiler_params=pltpu.CompilerParams(
            dimension_semantics=("arbitrary", "arbitrary"), vmem_limit_bytes=_vmem_limit(12 * tm * D * 4)),
        name="chan_dft",
    )(x, g.reshape(1, D), sh, sc, _bf16_table(cc), _bf16_table(sn))


def _pos_dft_kernel(w_ref, y_ref, o_ref, *, scale):
    o_ref[0] = (_dot(w_ref[...], y_ref[0]) * scale).astype(BF16)


def _pos_dft(y, L, tmo, tn):
    B, _, D = y.shape
    cl, sl = _dft_tables(L)
    w = _bf16_table(np.concatenate([cl, -sl], axis=1))
    scale = 1.0 / math.sqrt(L * GROUP_DIM)
    est = 2 * tmo * 2 * L * 2 + 2 * 2 * L * tn * 2 + 3 * tmo * tn * 4
    return pl.pallas_call(
        functools.partial(_pos_dft_kernel, scale=scale),
        grid=(B, D // tn, L // tmo),
        in_specs=[
            pl.BlockSpec((tmo, 2 * L), lambda b, j, i: (i, 0)),
            pl.BlockSpec((1, 2 * L, tn), lambda b, j, i: (b, 0, j)),
        ],
        out_specs=pl.BlockSpec((1, tmo, tn), lambda b, j, i: (b, i, j)),
        out_shape=jax.ShapeDtypeStruct((B, L, D), BF16),
        compiler_params=pltpu.CompilerParams(
            dimension_semantics=("arbitrary", "arbitrary", "arbitrary"), vmem_limit_bytes=_vmem_limit(est)),
        name="pos_dft",
    )(w, y)


FFT_N = 64
FFT_T = 16
FFT_PITCH = 24


def _fft_stage1_kernel(x_ref, g_ref, sh_ref, sc_ref, cc_ref, ns_ref, tw_ref, y_ref, gbuf_ref):
    gs = g_ref[...] * (1.0 + sc_ref[0])
    h = _norm_mod(x_ref[0].reshape(FFT_N * FFT_T, D_MODEL), gs, sh_ref[0]).astype(BF16)
    for g in range(N_GROUPS):
        cs = slice(g * GROUP_DIM, (g + 1) * GROUP_DIM)
        _put_groups(gbuf_ref.at[0], g * GROUP_DIM, _dot(h[:, cs], cc_ref[...]))
        _put_groups(gbuf_ref.at[1], g * GROUP_DIM, _dot(h[:, cs], ns_ref[...]))
    for t in range(FFT_T):
        sel = pl.ds(t, FFT_N, stride=FFT_PITCH)
        gst = jnp.concatenate([_get_cols(gbuf_ref.at[0], sel), _get_cols(gbuf_ref.at[1], sel)], axis=0)
        y = _dot(tw_ref[t], gst.astype(BF16))
        _put_cols(gbuf_ref.at[0], sel, y[:FFT_N])
        _put_cols(gbuf_ref.at[1], sel, y[FFT_N:])
    for p in range(2):
        y_ref[0, p] = _get_groups(gbuf_ref.at[p]).reshape(FFT_N, FFT_T, D_MODEL).astype(BF16)


def _get_cols(buf_ref, rows):
    return jnp.concatenate([buf_ref[c, rows, :] for c in range(buf_ref.shape[0])], axis=1)


def _put_cols(buf_ref, rows, val):
    for c in range(buf_ref.shape[0]):
        buf_ref[c, rows, :] = val[:, c * V7X_LANES:(c + 1) * V7X_LANES]


def _get_groups(buf_ref):
    val = _get_cols(buf_ref, slice(None))
    return val.reshape(FFT_N, FFT_PITCH, val.shape[1])[:, :FFT_T].reshape(FFT_N * FFT_T, val.shape[1])


def _put_groups(buf_ref, col0, val):
    pad = jnp.zeros((FFT_N, FFT_PITCH - FFT_T, val.shape[1]), val.dtype)
    val = jnp.concatenate([val.reshape(FFT_N, FFT_T, val.shape[1]), pad], axis=1)
    val = val.reshape(FFT_N * FFT_PITCH, val.shape[2])
    for c in range(val.shape[1] // V7X_LANES):
        buf_ref[col0 // V7X_LANES + c] = val[:, c * V7X_LANES:(c + 1) * V7X_LANES]


def _fft_stage2_kernel(y_ref, x_ref, w_ref, gate_ref, r_ref, o_ref, zbuf_ref, *, scale):
    @pl.when((pl.program_id(0) == 0) & (pl.program_id(1) == 0))
    def _():
        zbuf_ref[...] = jnp.zeros(zbuf_ref.shape, F32)

    for f in range(FFT_T):
        yst = jnp.concatenate([y_ref[0, 0, f], y_ref[0, 1, f]], axis=0)
        _put_cols(zbuf_ref, pl.ds(f, FFT_N, stride=FFT_PITCH), _dot(r_ref[...], yst) * scale)
    mix = _dot(_get_groups(zbuf_ref).astype(BF16), w_ref[...])
    x = x_ref[0].reshape(FFT_N * FFT_T, D_MODEL)
    o_ref[0] = (x + gate_ref[0] * mix).reshape(FFT_N, FFT_T, D_MODEL)


def _fnet_fft_layer(x, g, sh, sc, gate, fnet_w):
    B, L, D = x.shape
    n, tt = FFT_N, FFT_T
    x4 = x.reshape(B, n, n, D)
    cc, sn = _dft_tables(GROUP_DIM)
    f2 = np.arange(n)[None, :, None]
    tpos = np.arange(n)[:, None, None] + n * np.arange(n)[None, None, :]
    ang = 2.0 * np.pi * ((f2 * tpos) % L) / L
    c1, s1 = np.cos(ang), np.sin(ang)
    tw = np.concatenate([np.concatenate([c1, s1], axis=2), np.concatenate([-s1, c1], axis=2)], axis=1)
    c2, s2 = _dft_tables(n)
    r = np.concatenate([c2, s2], axis=1)
    ms = _mod_spec(sh.shape[0])
    tab = pl.BlockSpec((GROUP_DIM, GROUP_DIM), _CONST2)
    y = pl.pallas_call(
        _fft_stage1_kernel,
        grid=(B, n // tt),
        in_specs=[
            pl.BlockSpec((1, n, tt, D), lambda b, i: (b, 0, i, 0)),
            pl.BlockSpec((1, D), _CONST2), ms, ms, tab, tab,
            pl.BlockSpec((tt, 2 * n, 2 * n), lambda b, i: (i, 0, 0)),
        ],
        out_specs=pl.BlockSpec((1, 2, n, tt, D), lambda b, i: (b, 0, 0, i, 0)),
        out_shape=jax.ShapeDtypeStruct((B, 2, n, n, D), BF16),
        scratch_shapes=[pltpu.VMEM((2, D // V7X_LANES, n * FFT_PITCH, V7X_LANES), F32)],
        compiler_params=pltpu.CompilerParams(
            dimension_semantics=("arbitrary", "arbitrary"),
            vmem_limit_bytes=_vmem_limit(2 * n * tt * D * (4 + 4 + 4) + 2 * n * tt * D * 4)),
        name="fft_stage1",
    )(x4, g.reshape(1, D), sh, sc, _bf16_table(cc), _bf16_table(-sn), _bf16_table(tw))
    out = pl.pallas_call(
        functools.partial(_fft_stage2_kernel, scale=1.0 / math.sqrt(L * GROUP_DIM)),
        grid=(B, n // tt),
        in_specs=[
            pl.BlockSpec((1, 2, tt, n, D), lambda b, j: (b, 0, j, 0, 0)),
            pl.BlockSpec((1, n, tt, D), lambda b, j: (b, 0, j, 0)),
            pl.BlockSpec((D, D), _CONST2),
            _mod_spec(gate.shape[0]),
            pl.BlockSpec((n, 2 * n), _CONST2),
        ],
        out_specs=pl.BlockSpec((1, n, tt, D), lambda b, j: (b, 0, j, 0)),
        out_shape=jax.ShapeDtypeStruct((B, n, n, D), F32),
        scratch_shapes=[pltpu.VMEM((D // V7X_LANES, n * FFT_PITCH, V7X_LANES), F32)],
        compiler_params=pltpu.CompilerParams(
            dimension_semantics=("arbitrary", "arbitrary"),
            vmem_limit_bytes=_vmem_limit(n * tt * D * (4 + 8 + 8 + 4 + 4) + 2 * D * D * 2)),
        name="fft_stage2",
    )(y, x4, fnet_w, gate, _bf16_table(r))
    return out.reshape(B, L, D)


def _trunk(x, mods, ctx_k, ctx_v, wts, tm):
    (norm_mix, norm_ffn, w_qkv, q_norm, k_norm, sink, w_o, pool_w, pool_scale, fnet_w,
     w_in, conv_w, conv_b, w_out) = wts
    B, L, D = x.shape
    latent = ctx_k is not None
    ks, vs = [], []
    for i in range(DEPTH):
        kind, j = i % N_MIXERS, i // N_MIXERS
        sh1, sc1, g1, sh2, sc2, g2 = mods[i]
        if kind == 0:
            outs = _qkv_layer(x, norm_mix[i], sh1, sc1, w_qkv[j], q_norm[j], k_norm[j], tm, rope=latent)
            q, k, v = outs[:3]
            if latent:
                x = _attention(x, g1, w_o[j], q, k, v, sink[j], ctx_k[j], ctx_v[j])
            else:
                x = _attention(x, g1, w_o[j], q, k, v, sink[j])
                ks.append(outs[3].reshape(B, L, N_KV_HEADS, HEAD_DIM))
                vs.append(outs[4].reshape(B, L, N_KV_HEADS, HEAD_DIM))
        elif kind == 1:
            x = _pool_layer(x, norm_mix[i], sh1, sc1, g1, pool_w[j], pool_scale[j])
        elif L == FFT_N * FFT_N:
            x = _fnet_fft_layer(x, norm_mix[i], sh1, sc1, g1, fnet_w[j])
        else:
            y = _chan_dft(x, norm_mix[i], sh1, sc1, tm).reshape(B, 2 * L, D)
            f = _pos_dft(y, L, min(L, 512), 512 if L > 512 else D)
            x = _proj_res(x, f, fnet_w[j], g1, tm)
        x = _ffn_layer(x, norm_ffn[i], sh2, sc2, g2, w_in[i], conv_w[i], conv_b[i], w_out[i], tm)
    return x, ks, vs


def kernel(x_prompt, x_sample, cache_k, cache_v, c, c_ctx, norm_mix, norm_ffn, ada_w, ada_b, attn_w_qkv,
           attn_q_norm, attn_k_norm, attn_sink, attn_w_o, pool_w, pool_scale, fnet_w, ffn_w_in, ffn_conv_w,
           ffn_conv_b, ffn_w_out):
    nb = c.shape[0]
    cond = jnp.concatenate([c, c_ctx[None], jnp.zeros((ADA_ROWS - nb - 1, D_MODEL), F32)], axis=0)
    mods = _ada_mods(cond, ada_w, ada_b).reshape(DEPTH, ADA_ROWS, 6, 1, D_MODEL)
    mods_sample = [[mods[i, :nb, m] for m in range(6)] for i in range(DEPTH)]
    mods_prompt = [[mods[i, nb:nb + 1, m] for m in range(6)] for i in range(DEPTH)]
    wts = (norm_mix, norm_ffn, attn_w_qkv.astype(BF16), attn_q_norm, attn_k_norm, attn_sink,
           attn_w_o.astype(BF16), pool_w.astype(BF16), pool_scale, fnet_w.astype(BF16),
           ffn_w_in.astype(BF16), ffn_conv_w, ffn_conv_b, ffn_w_out.astype(BF16))
    y_prompt, ks, vs = _trunk(x_prompt, mods_prompt, None, None, wts, tm=256)
    past = cache_k.shape[2]
    ctx_k = [cache_k[:, j].reshape(nb, past, KV_DIM).astype(BF16) for j in range(cache_k.shape[1])]
    ctx_v = [cache_v[:, j].reshape(nb, past, KV_DIM).astype(BF16).transpose(0, 2, 1) for j in range(cache_v.shape[1])]
    y_sample, _, _ = _trunk(x_sample, mods_sample, ctx_k, ctx_v, wts, tm=512)
    return (y_prompt, y_sample, jnp.stack(ks, axis=1), jnp.stack(vs, axis=1))
```
